```python
import jax, jax.numpy as jnp
from jax import lax
import numpy as np

D_MODEL = 1024
BATCH = 32
SEQ = 2048
DEPTH = 4

CHUNK = 64
Q_BLOCK = 128
HEAD_DIM = 64
N_BRANCH = 4
BRANCH_W = 256
NEG_INF = -1e30
A_HEADS = 4
A_W = A_HEADS * HEAD_DIM
B_HEADS = 4
B_W = B_HEADS * HEAD_DIM
C_HEADS = 4
C_Q_RANK = 256
C_KV_RANK = 128
C_NOPE = 64
C_ROPE = 32
C_V = 64
C_QK = C_NOPE + C_ROPE
ROPE_BASE = 10000.0
D_HEADS = 4
D_W = D_HEADS * HEAD_DIM
D_LEFT_CHUNKS = 8
D_BAND = (D_LEFT_CHUNKS + 1) * CHUNK
D_MAX_REL = 128
A_COLS = 3 * A_W + A_HEADS
B_COLS = 3 * B_W
C_COLS = C_Q_RANK + C_KV_RANK + C_ROPE
D_COLS = 3 * D_W
G_COLS = N_BRANCH * D_MODEL
IN_COLS = A_COLS + B_COLS + C_COLS + D_COLS + G_COLS
SPLITS = (A_COLS, A_COLS + B_COLS, A_COLS + B_COLS + C_COLS, A_COLS + B_COLS + C_COLS + D_COLS)
N_EXPERTS = 16
N_GROUPS = 4
EXPERTS_PER_GROUP = N_EXPERTS // N_GROUPS
TOP_K = 2
D_EXPERT = 256
D_SHARED = 256

kernel_name = "hybrid_chunk_causal_moe_trunk"


def rms_norm(x, g, eps=1e-6):
    x32 = x.astype(jnp.float32)
    y = x32 * lax.rsqrt(jnp.mean(x32 * x32, axis=-1, keepdims=True) + eps)
    return (y * g.astype(jnp.float32)).astype(x.dtype)


def rope_angles(positions):
    half = C_ROPE // 2
    inv_freq = jnp.power(ROPE_BASE, -jnp.arange(half, dtype=jnp.float32) / half)
    return positions.astype(jnp.float32)[:, :, None, None] * inv_freq


def apply_rope(t, ang):
    half = t.shape[-1] // 2
    cos, sin = jnp.cos(ang), jnp.sin(ang)
    t1, t2 = t[..., :half].astype(jnp.float32), t[..., half:].astype(jnp.float32)
    return jnp.concatenate([t1 * cos - t2 * sin, t1 * sin + t2 * cos], axis=-1).astype(t.dtype)


def swept_attention(q, k, v, weights_fn):
    B, S, H, _ = q.shape
    d_v = v.shape[-1]
    outs = []
    for blk in range(S // Q_BLOCK):
        q0, end = blk * Q_BLOCK, (blk + 1) * Q_BLOCK
        s = jnp.einsum('bqhd,bkhd->bhqk', q[:, q0:end], k[:, :end]).astype(jnp.float32)
        t_idx = jnp.arange(q0, end)[:, None]
        s_idx = jnp.arange(end)[None, :]
        w = weights_fn(s, t_idx, s_idx, q0, end)
        outs.append(jnp.einsum('bhqk,bkhd->bqhd', w.astype(v.dtype), v[:, :end]))
    return jnp.concatenate(outs, axis=1).reshape(B, S, H * d_v)


def chunk_band_attention(q, k, v, rel_bias):
    B, S, H, d = q.shape
    n = S // CHUNK
    left = D_LEFT_CHUNKS * CHUNK

    def band(t):
        tp = jnp.pad(t, ((0, 0), (left, 0), (0, 0), (0, 0))).reshape(B, n + D_LEFT_CHUNKS, CHUNK, H, d)
        return jnp.concatenate([tp[:, i:i + n] for i in range(D_LEFT_CHUNKS + 1)], axis=2)

    kb, vb = band(k), band(v)
    qc = q.reshape(B, n, CHUNK, H, d)
    s = jnp.einsum('bnqhd,bnkhd->bnhqk', qc, kb).astype(jnp.float32) * d ** -0.5
    dist = jnp.arange(CHUNK)[:, None] + left - jnp.arange(D_BAND)[None, :]
    bias = rel_bias[:, jnp.clip(dist, -D_MAX_REL, D_MAX_REL) + D_MAX_REL].astype(jnp.float32)
    key_pos = jnp.arange(n)[:, None] * CHUNK - left + jnp.arange(D_BAND)[None, :]
    valid = (key_pos >= 0)[None, :, None, None, :]
    w = jax.nn.softmax(jnp.where(valid, s + bias, NEG_INF), axis=-1)
    o = jnp.einsum('bnhqk,bnkhd->bnqhd', w.astype(v.dtype), vb)
    return o.reshape(B, S, H * d)


def token_mixers(h, positions, w_in, fox_forget_b, fox_q_g, fox_k_g, mla_cq_g, mla_ckv_g,
                 mla_w_uq, mla_w_ukv, mla_q_g, mla_k_g, chunk_q_g, chunk_k_g, chunk_rel_bias,
                 w_branch, w_out):
    B, S, _ = h.shape
    proj = h @ w_in
    pa, pb, pc, pd, pg = jnp.split(proj, SPLITS, axis=-1)

    def heads(t, n_heads):
        return t.reshape(B, S, n_heads, -1)

    qa, ka, va, fa = jnp.split(pa, [A_W, 2 * A_W, 3 * A_W], axis=-1)
    qa = rms_norm(heads(qa, A_HEADS), fox_q_g)
    ka = rms_norm(heads(ka, A_HEADS), fox_k_g)
    va = heads(va, A_HEADS)
    log_f = jax.nn.log_sigmoid(fa.astype(jnp.float32) + fox_forget_b.astype(jnp.float32))
    cum_f = jnp.transpose(jnp.cumsum(log_f, axis=1), (0, 2, 1))

    def fox_weights(s, t_idx, s_idx, q0, end):
        logits = s * HEAD_DIM ** -0.5 + cum_f[:, :, q0:end, None] - cum_f[:, :, None, :end]
        return jax.nn.softmax(jnp.where(s_idx <= t_idx, logits, NEG_INF), axis=-1)

    oa = swept_attention(qa, ka, va, fox_weights)

    qb, kb, vb = [heads(t, B_HEADS) for t in jnp.split(pb, [B_W, 2 * B_W], axis=-1)]

    def stick_weights(s, t_idx, s_idx, q0, end):
        z = s * HEAD_DIM ** -0.5
        strict = s_idx < t_idx
        log_keep = jnp.where(strict, jax.nn.log_sigmoid(-z), 0.0)
        after = lax.cumsum(log_keep, axis=3, reverse=True) - log_keep
        return jnp.where(strict, jnp.exp(jax.nn.log_sigmoid(z) + after), 0.0)

    ob = swept_attention(qb, kb, vb, stick_weights)

    cq, ckv, kr = jnp.split(pc, [C_Q_RANK, C_Q_RANK + C_KV_RANK], axis=-1)
    q_c = heads(rms_norm(cq, mla_cq_g) @ mla_w_uq, C_HEADS)
    kv_c = heads(rms_norm(ckv, mla_ckv_g) @ mla_w_ukv, C_HEADS)
    ang = rope_angles(positions)
    q_c = jnp.concatenate([q_c[..., :C_NOPE], apply_rope(q_c[..., C_NOPE:], ang)], axis=-1)
    k_rope = apply_rope(kr[:, :, None, :], ang)
    k_c = jnp.concatenate([kv_c[..., :C_NOPE], jnp.broadcast_to(k_rope, (B, S, C_HEADS, C_ROPE))], axis=-1)
    v_c = kv_c[..., C_NOPE:]
    q_c = rms_norm(q_c, mla_q_g)
    k_c = rms_norm(k_c, mla_k_g)

    def chunk_causal_weights(s, t_idx, s_idx, q0, end):
        logits = jnp.where(s_idx // CHUNK <= t_idx // CHUNK, s * C_QK ** -0.5, NEG_INF)
        return jax.nn.softmax(logits, axis=-1)

    oc = swept_attention(q_c, k_c, v_c, chunk_causal_weights)

    qd, kd, vd = [heads(t, D_HEADS) for t in jnp.split(pd, [D_W, 2 * D_W], axis=-1)]
    od = chunk_band_attention(rms_norm(qd, chunk_q_g), rms_norm(kd, chunk_k_g), vd, chunk_rel_bias)

    gates = jax.nn.sigmoid(pg.astype(jnp.float32)).astype(h.dtype).reshape(B, S, N_BRANCH, D_MODEL)
    branches = (oa, ob, oc, od)
    merged = gates[:, :, 0] * (branches[0] @ w_branch[0])
    for i in range(1, N_BRANCH):
        merged = merged + gates[:, :, i] * (branches[i] @ w_branch[i])
    return merged @ w_out


def swiglu(h, w_gate, w_up, w_down):
    return (jax.nn.silu(h @ w_gate) * (h @ w_up)) @ w_down


def moe_ffn(h, router_w, router_b, w_gate, w_up, w_down, sh_gate, sh_up, sh_down):
    scores = jax.nn.sigmoid((h @ router_w).astype(jnp.float32))
    sel = (scores + router_b.astype(jnp.float32)).reshape(*scores.shape[:-1], N_GROUPS, EXPERTS_PER_GROUP)
    group_score = lax.top_k(sel, 2)[0].sum(-1)
    grp = jnp.argmax(group_score, axis=-1)
    in_group = jnp.take_along_axis(sel, grp[..., None, None], axis=-2)[..., 0, :]
    _, local = lax.top_k(in_group, TOP_K)
    idx = grp[..., None] * EXPERTS_PER_GROUP + local
    w = jnp.take_along_axis(scores, idx, axis=-1)
    w = w / jnp.sum(w, axis=-1, keepdims=True)
    combine = jnp.sum(jax.nn.one_hot(idx, N_EXPERTS, dtype=jnp.float32) * w[..., None], axis=-2).astype(h.dtype)
    out = swiglu(h, sh_gate, sh_up, sh_down)
    for e in range(N_EXPERTS):
        out = out + combine[..., e:e + 1] * swiglu(h, w_gate[e], w_up[e], w_down[e])
    return out


def setup_inputs(seed: int = 0) -> dict:
    key = jax.random.key(seed)
    ks = iter(jax.random.split(key, 40))
    f32 = jnp.float32
    L = DEPTH

    def nrm(shape, scale):
        return scale * jax.random.normal(next(ks), shape, f32)

    def gain(shape):
        return 1.0 + 0.05 * jax.random.normal(next(ks), shape, f32)

    x = jax.random.normal(next(ks), (BATCH, SEQ, D_MODEL), f32)
    c = jax.random.normal(next(ks), (BATCH, D_MODEL), f32)
    offsets = jax.random.randint(next(ks), (BATCH, 1), 0, 64, dtype=jnp.int32) * CHUNK
    positions = (offsets + jnp.arange(SEQ, dtype=jnp.int32)[None, :]).astype(jnp.int32)
    return {
        "x": x,
        "c": c,
        "positions": positions,
        "norm_mix_g": gain((L, D_MODEL)),
        "norm_ffn_g": gain((L, D_MODEL)),
        "w_ada": nrm((L, D_MODEL, 6 * D_MODEL), 0.5 * D_MODEL ** -0.5),
        "b_ada": nrm((L, 6 * D_MODEL), 0.01),
        "w_in": nrm((L, D_MODEL, IN_COLS), D_MODEL ** -0.5),
        "fox_forget_b": jax.random.uniform(next(ks), (L, A_HEADS), f32, minval=1.0, maxval=5.0),
        "fox_q_g": gain((L, HEAD_DIM)),
        "fox_k_g": gain((L, HEAD_DIM)),
        "mla_cq_g": gain((L, C_Q_RANK)),
        "mla_ckv_g": gain((L, C_KV_RANK)),
        "mla_w_uq": nrm((L, C_Q_RANK, C_HEADS * C_QK), C_Q_RANK ** -0.5),
        "mla_w_ukv": nrm((L, C_KV_RANK, C_HEADS * (C_NOPE + C_V)), C_KV_RANK ** -0.5),
        "mla_q_g": gain((L, C_QK)),
        "mla_k_g": gain((L, C_QK)),
        "chunk_q_g": gain((L, HEAD_DIM)),
        "chunk_k_g": gain((L, HEAD_DIM)),
        "chunk_rel_bias": nrm((L, D_HEADS, 2 * D_MAX_REL + 1), 0.5),
        "w_branch": nrm((L, N_BRANCH, BRANCH_W, D_MODEL), BRANCH_W ** -0.5),
        "w_out": nrm((L, D_MODEL, D_MODEL), D_MODEL ** -0.5),
        "router_w": nrm((D_MODEL, N_EXPERTS), D_MODEL ** -0.5),
        "router_b": nrm((N_EXPERTS,), 0.01),
        "exp_w_gate": nrm((L, N_EXPERTS, D_MODEL, D_EXPERT), D_MODEL ** -0.5),
        "exp_w_up": nrm((L, N_EXPERTS, D_MODEL, D_EXPERT), D_MODEL ** -0.5),
        "exp_w_down": nrm((L, N_EXPERTS, D_EXPERT, D_MODEL), D_EXPERT ** -0.5),
        "sh_w_gate": nrm((L, D_MODEL, D_SHARED), D_MODEL ** -0.5),
        "sh_w_up": nrm((L, D_MODEL, D_SHARED), D_MODEL ** -0.5),
        "sh_w_down": nrm((L, D_SHARED, D_MODEL), D_SHARED ** -0.5),
    }


def reference(x, c, positions, norm_mix_g, norm_ffn_g, w_ada, b_ada, w_in, fox_forget_b,
              fox_q_g, fox_k_g, mla_cq_g, mla_ckv_g, mla_w_uq, mla_w_ukv, mla_q_g, mla_k_g,
              chunk_q_g, chunk_k_g, chunk_rel_bias, w_branch, w_out, router_w, router_b,
              exp_w_gate, exp_w_up, exp_w_down, sh_w_gate, sh_w_up, sh_w_down):
    cond = jax.nn.silu(c)
    for l in range(DEPTH):
        mod = (cond @ w_ada[l] + b_ada[l])[:, None, :]
        sh_m, sc_m, g_m, sh_f, sc_f, g_f = jnp.split(mod, 6, axis=-1)
        h = rms_norm(x, norm_mix_g[l]) * (1.0 + sc_m) + sh_m
        x = x + g_m * token_mixers(h, positions, w_in[l], fox_forget_b[l], fox_q_g[l], fox_k_g[l],
                                   mla_cq_g[l], mla_ckv_g[l], mla_w_uq[l], mla_w_ukv[l], mla_q_g[l],
                                   mla_k_g[l], chunk_q_g[l], chunk_k_g[l], chunk_rel_bias[l],
                                   w_branch[l], w_out[l])
        h = rms_norm(x, norm_ffn_g[l]) * (1.0 + sc_f) + sh_f
        x = x + g_f * moe_ffn(h, router_w, router_b, exp_w_gate[l], exp_w_up[l], exp_w_down[l],
                              sh_w_gate[l], sh_w_up[l], sh_w_down[l])
    return x
```

```python
import functools
import math

import numpy as np
import jax
import jax.numpy as jnp
from jax import lax
from jax.experimental import pallas as pl
from jax.experimental.pallas import tpu as pltpu

F32 = jnp.float32
BF16 = jnp.bfloat16

HEAD_DIM = 64
N_HEADS = 4
MIX_W = N_HEADS * HEAD_DIM
CHUNK = 64
NEG_INF = -1e30
EPS = 1e-6
C_Q_RANK = 256
C_KV_RANK = 128
C_NOPE = 64
C_ROPE = 32
C_HALF = C_ROPE // 2
C_QK = C_NOPE + C_ROPE
ROPE_BASE = 10000.0
D_LEFT_CHUNKS = 8
D_MAX_REL = 128
N_BRANCH = 4
N_EXPERTS = 16
N_GROUPS = 4
EXPERTS_PER_GROUP = 4
D_EXPERT = 256

LANES = 128
ATT_BLOCK = 128
PAIR_W = 2 * HEAD_DIM
VMEM_LIMIT = 56 * 1024 * 1024

P1_COLS = 9 * MIX_W + C_Q_RANK + C_KV_RANK + LANES


def _cparams(sem):
    return pltpu.CompilerParams(dimension_semantics=sem, vmem_limit_bytes=VMEM_LIMIT)


def _dot(a, b):
    return jnp.dot(a, b, preferred_element_type=F32)


def _dot_nt(a, b):
    return lax.dot_general(a, b, (((1,), (1,)), ((), ())), preferred_element_type=F32)


def _sigmoid(x):
    return 1.0 / (1.0 + jnp.exp(-x))


def _rms(x, g):
    return x * lax.rsqrt(jnp.mean(x * x, axis=-1, keepdims=True) + EPS) * g


def _mod_kernel(c_ref, w_ref, b_ref, o_ref):
    c = c_ref[...]
    cond = c * _sigmoid(c)
    o_ref[0] = jnp.dot(cond, w_ref[0], preferred_element_type=F32,
                       precision=lax.Precision.HIGHEST) + b_ref[0]


def _modulation(c, w_ada, b_ada):
    L, D, D6 = w_ada.shape
    B = c.shape[0]
    tn = 1024
    return pl.pallas_call(
        _mod_kernel,
        out_shape=jax.ShapeDtypeStruct((L, B, D6), F32),
        grid=(L, D6 // tn),
        in_specs=[pl.BlockSpec((B, D), lambda l, j: (0, 0)),
                  pl.BlockSpec((1, D, tn), lambda l, j: (l, 0, j)),
                  pl.BlockSpec((1, 1, tn), lambda l, j: (l, 0, j))],
        out_specs=pl.BlockSpec((1, B, tn), lambda l, j: (l, 0, j)),
        name="adaln_mod",
        compiler_params=_cparams(("arbitrary", "arbitrary")),
    )(c, w_ada, b_ada.reshape(L, 1, D6))


def _rope_kernel(ang_ref, sign_ref, cos_ref, sin_ref):
    a = ang_ref[0]
    cos_ref[0] = jnp.cos(a)
    sin_ref[0] = jnp.sin(a) * sign_ref[...]


def _rope_tables(positions):
    B, S = positions.shape
    inv_freq = jnp.power(ROPE_BASE, -jnp.arange(C_HALF, dtype=F32) / C_HALF)
    inv_l = jnp.tile(inv_freq, LANES // C_HALF)
    ang = positions.astype(F32)[:, :, None] * inv_l[None, None, :]
    sign = jnp.where(jnp.arange(LANES) < LANES // 2, -1.0, 1.0).astype(F32).reshape(1, LANES)
    ts = min(S, 1024)
    spec = pl.BlockSpec((1, ts, LANES), lambda b, i: (b, i, 0))
    return pl.pallas_call(
        _rope_kernel,
        out_shape=(jax.ShapeDtypeStruct((B, S, LANES), F32),) * 2,
        grid=(B, S // ts),
        in_specs=[spec, pl.BlockSpec((1, LANES), lambda b, i: (0, 0))],
        out_specs=(spec, spec),
        name="rope_tables",
        compiler_params=_cparams(("arbitrary", "arbitrary")),
    )(ang, sign)


def _rope(t, cos, sin):
    return t * cos + pltpu.roll(t, LANES // 2, 1) * sin


def _p1_kernel(x_ref, mod_ref, ng_ref, w1_ref, wfa_ref, fb_ref, gv_ref, wuq_ref, wukv_ref,
               bd_ref, mp_ref, cos_ref, sin_ref,
               qa_ref, ka_ref, va_ref, lf_ref, qb_ref, kb_ref, vb_ref,
               qc_ref, kc_ref, vc_ref, qd_ref, kd_ref, vd_ref):
    x = x_ref[0]
    sh = mod_ref[0, 0:1, :]
    sc = mod_ref[0, 1:2, :]
    hb = (_rms(x, ng_ref[...]) * (1.0 + sc) + sh).astype(BF16)
    W = MIX_W

    def mm(c0, c1):
        return _dot(hb, w1_ref[:, c0:c1])

    def head_norm(t, g):
        ss = _dot((t * t).astype(BF16), bd_ref[...])
        return (t * lax.rsqrt(ss * (1.0 / HEAD_DIM) + EPS) * g).astype(BF16)

    r = mm(0, 3 * W)
    qa_ref[0] = head_norm(r[:, 0:W], gv_ref[0:1, 0:W])
    ka_ref[0] = head_norm(r[:, W:2 * W], gv_ref[1:2, 0:W])
    va_ref[0] = r[:, 2 * W:3 * W].astype(BF16)
    fa = _dot_nt(wfa_ref[...], hb) + fb_ref[...]
    lf_ref[0] = jnp.minimum(fa, 0.0) - jnp.log(1.0 + jnp.exp(-jnp.abs(fa)))

    r = mm(3 * W, 6 * W)
    qb_ref[0] = r[:, 0:W].astype(BF16)
    kb_ref[0] = r[:, W:2 * W].astype(BF16)
    vb_ref[0] = r[:, 2 * W:3 * W].astype(BF16)

    r = mm(6 * W, 9 * W)
    qd_ref[0] = head_norm(r[:, 0:W], gv_ref[2:3, 0:W])
    kd_ref[0] = head_norm(r[:, W:2 * W], gv_ref[3:4, 0:W])
    vd_ref[0] = r[:, 2 * W:3 * W].astype(BF16)

    c0 = 9 * W
    r = mm(c0, c0 + C_Q_RANK + C_KV_RANK + LANES)
    cos = cos_ref[0]
    sin = sin_ref[0]
    cq = _rms(r[:, 0:C_Q_RANK], gv_ref[4:5, 0:C_Q_RANK]).astype(BF16)
    ckv = _rms(r[:, C_Q_RANK:C_Q_RANK + C_KV_RANK], gv_ref[5:6, 0:C_KV_RANK]).astype(BF16)
    kr = _rope(r[:, C_Q_RANK + C_KV_RANK:], cos, sin)
    qu = _dot(cq, wuq_ref[...])
    kvu = _dot(ckv, wukv_ref[...])
    vc_ref[0] = kvu[:, 2 * LANES:].astype(BF16)
    for p in range(2):
        qp = jnp.concatenate([qu[:, p * 2 * LANES:p * 2 * LANES + LANES],
                              _rope(qu[:, p * 2 * LANES + LANES:(p + 1) * 2 * LANES], cos, sin)], axis=1)
        kp = jnp.concatenate([kvu[:, p * LANES:(p + 1) * LANES], kr], axis=1)
        for t, gr, dst in ((qp, 6, qc_ref), (kp, 7, kc_ref)):
            ss = _dot((t * t).astype(BF16), mp_ref[p])
            tn = t * lax.rsqrt(ss * (1.0 / C_QK) + EPS) * gv_ref[gr:gr + 1, p * 2 * LANES:(p + 1) * 2 * LANES]
            dst[0, :, p * 2 * LANES:(p + 1) * 2 * LANES] = tn.astype(BF16)


def _p1(x, mod, ng, lw, cosl, sinl, tm):
    B, S, D = x.shape
    W = MIX_W
    bs = lambda w: pl.BlockSpec((1, tm, w), lambda b, i: (b, i, 0))
    full = lambda a: pl.BlockSpec(a.shape, lambda b, i: (0,) * a.ndim)
    o256 = jax.ShapeDtypeStruct((B, S, W), BF16)
    o512 = jax.ShapeDtypeStruct((B, S, 2 * W), BF16)
    out_shape = (o256, o256, o256, jax.ShapeDtypeStruct((B, 8, S), F32),
                 o256, o256, o256, o512, o512, o256, o256, o256, o256)
    out_specs = (bs(W), bs(W), bs(W), pl.BlockSpec((1, 8, tm), lambda b, i: (b, 0, i)),
                 bs(W), bs(W), bs(W), bs(2 * W), bs(2 * W), bs(W), bs(W), bs(W), bs(W))
    consts = (ng, lw["w1"], lw["wfa"], lw["fb"], lw["gv"], lw["wuq"], lw["wukv"], lw["bd"], lw["mp"])
    return pl.pallas_call(
        _p1_kernel,
        out_shape=out_shape,
        grid=(B, S // tm),
        in_specs=[bs(D), pl.BlockSpec((1, 6, D), lambda b, i: (b, 0, 0))]
                 + [full(a) for a in consts] + [bs(LANES), bs(LANES)],
        out_specs=out_specs,
        name="in_proj",
        compiler_params=_cparams(("arbitrary", "arbitrary")),
    )(x, mod, *consts, cosl, sinl)


def _lane_cumsum(x):
    n = x.shape[1]
    idx = lax.broadcasted_iota(jnp.int32, x.shape, 1)
    k = 1
    while k < n:
        x = x + jnp.where(idx >= k, pltpu.roll(x, k, 1), 0.0)
        k *= 2
    return x


def _softmax_attn_kernel(mode, S, *refs):
    T = ATT_BLOCK
    if mode == "fox":
        q_ref, k_ref, v_ref, hm_ref, lf_ref, o_ref, cf_ref = refs
        cf = _lane_cumsum(lf_ref[0])
        for h in range(N_HEADS):
            cf_ref[h] = cf[h:h + 1, :]
    elif mode == "band":
        q_ref, k_ref, v_ref, hm_ref, bias_ref, o_ref = refs
    else:
        q_ref, k_ref, v_ref, hm_ref, o_ref = refs
    pair = pl.program_id(1)
    row = lax.broadcasted_iota(jnp.int32, (T, T), 0)
    col = lax.broadcasted_iota(jnp.int32, (T, T), 1)
    if mode == "fox":
        dmask = col <= row
    elif mode == "mla":
        dmask = (col // CHUNK) <= (row // CHUNK)
    lane = lax.broadcasted_iota(jnp.int32, (T, PAIR_W), 1)
    nkb = D_LEFT_CHUNKS * CHUNK // T

    def q_block(qi, _):
        q0 = pl.multiple_of(qi * T, T)
        qp = q_ref[0, pl.ds(q0, T), :]
        qm = [qp * hm_ref[0, a:a + 1, :] for a in range(2)]

        def step(kj, carry, diag):
            k0 = pl.multiple_of(kj * T, T)
            kt = k_ref[0, pl.ds(k0, T), :]
            vt = v_ref[0, pl.ds(k0, T), :]
            new = []
            for a in range(2):
                m, l, acc = carry[a]
                s = _dot_nt(qm[a], kt)
                if mode == "fox":
                    s = s - cf_ref[2 * pair + a, :, pl.ds(k0, T)]
                if mode == "band":
                    s = s + bias_ref[a, kj - qi + nkb]
                elif diag:
                    s = jnp.where(dmask, s, NEG_INF)
                m_new = jnp.maximum(m, jnp.max(s, axis=-1, keepdims=True))
                alpha = jnp.exp(m - m_new)
                pe = jnp.exp(s - m_new)
                l = alpha * l + jnp.sum(pe, axis=-1, keepdims=True)
                acc = alpha * acc + _dot(pe.astype(BF16), vt)
                new.append((m_new, l, acc))
            return tuple(new)

        init = tuple((jnp.full((T, 1), NEG_INF, F32), jnp.zeros((T, 1), F32),
                      jnp.zeros((T, PAIR_W), F32)) for _ in range(2))
        if mode == "band":
            carry = lax.fori_loop(jnp.maximum(qi - nkb, 0), qi + 1,
                                  lambda kj, c: step(kj, c, False), init)
        else:
            carry = lax.fori_loop(0, qi, lambda kj, c: step(kj, c, False), init)
            carry = step(qi, carry, True)
        (_, l0, a0), (_, l1, a1) = carry
        o = jnp.where(lane < HEAD_DIM, a0 / l0, a1 / l1)
        o_ref[0, pl.ds(q0, T), :] = o.astype(o_ref.dtype)
        return 0

    lax.fori_loop(0, S // T, q_block, 0)


def _softmax_attn(mode, q, k, v, hm, extra=None):
    B, S, _ = v.shape
    cw = q.shape[2] // 2
    qspec = pl.BlockSpec((1, S, cw), lambda b, p: (b, 0, p))
    vspec = pl.BlockSpec((1, S, PAIR_W), lambda b, p: (b, 0, p))
    in_specs = [qspec, qspec, vspec, pl.BlockSpec((1, 2, cw), lambda b, p: (p, 0, 0))]
    args = [q, k, v, hm]
    scratch = []
    if mode == "fox":
        in_specs.append(pl.BlockSpec((1, 8, S), lambda b, p: (b, 0, 0)))
        args.append(extra)
        scratch = [pltpu.VMEM((N_HEADS, 1, S), F32)]
    elif mode == "band":
        nb = extra.shape[1]
        in_specs.append(pl.BlockSpec((2, nb, ATT_BLOCK, ATT_BLOCK), lambda b, p: (p, 0, 0, 0)))
        args.append(extra)
    return pl.pallas_call(
        functools.partial(_softmax_attn_kernel, mode, S),
        out_shape=jax.ShapeDtypeStruct((B, S, MIX_W), BF16),
        grid=(B, 2),
        in_specs=in_specs,
        out_specs=vspec,
        scratch_shapes=scratch,
        name="attn_" + mode,
        compiler_params=_cparams(("arbitrary", "arbitrary")),
    )(*args)


def _stick_kernel(S, q_ref, k_ref, v_ref, hm_ref, uo_ref, o_ref):
    T = ATT_BLOCK
    row = lax.broadcasted_iota(jnp.int32, (T, T), 0)
    col = lax.broadcasted_iota(jnp.int32, (T, T), 1)
    strict = col < row
    lane = lax.broadcasted_iota(jnp.int32, (T, PAIR_W), 1)

    def q_block(qi, _):
        q0 = pl.multiple_of(qi * T, T)
        qp = q_ref[0, pl.ds(q0, T), :]
        qm = [qp * hm_ref[0, a:a + 1, :] for a in range(2)]

        def step(kj, carry, diag):
            k0 = pl.multiple_of(kj * T, T)
            kt = k_ref[0, pl.ds(k0, T), :]
            vt = v_ref[0, pl.ds(k0, T), :]
            new = []
            for a in range(2):
                acc, cs = carry[a]
                z = _dot_nt(qm[a], kt)
                sp = jnp.maximum(z, 0.0) + jnp.log(1.0 + jnp.exp(-jnp.abs(z)))
                if diag:
                    sp = jnp.where(strict, sp, 0.0)
                hi = sp.astype(BF16)
                lo = (sp - hi.astype(F32)).astype(BF16)
                cum = _dot(jnp.concatenate([hi, lo], axis=1), uo_ref[...])
                w = jnp.exp(z - sp - cum[:, :T] - cs)
                if diag:
                    w = jnp.where(strict, w, 0.0)
                acc = acc + _dot(w.astype(BF16), vt)
                cs = cs + cum[:, T:]
                new.append((acc, cs))
            return tuple(new)

        init = tuple((jnp.zeros((T, PAIR_W), F32), jnp.zeros((T, T), F32)) for _ in range(2))
        carry = step(qi, init, True)
        carry = lax.fori_loop(0, qi, lambda i, c: step(qi - 1 - i, c, False), carry)
        (a0, _), (a1, _) = carry
        o = jnp.where(lane < HEAD_DIM, a0, a1)
        o_ref[0, pl.ds(q0, T), :] = o.astype(o_ref.dtype)
        return 0

    lax.fori_loop(0, S // T, q_block, 0)


def _stick_attn(q, k, v, hm, uo):
    B, S, _ = v.shape
    spec = pl.BlockSpec((1, S, PAIR_W), lambda b, p: (b, 0, p))
    return pl.pallas_call(
        functools.partial(_stick_kernel, S),
        out_shape=jax.ShapeDtypeStruct((B, S, MIX_W), BF16),
        grid=(B, 2),
        in_specs=[spec, spec, spec, pl.BlockSpec((1, 2, PAIR_W), lambda b, p: (p, 0, 0)),
                  pl.BlockSpec(uo.shape, lambda b, p: (0, 0))],
        out_specs=spec,
        name="attn_stick",
        compiler_params=_cparams(("arbitrary", "arbitrary")),
    )(q, k, v, hm, uo)


def _route(scores_t, sel_t):
    G, E = N_GROUPS, EXPERTS_PER_GROUP
    sel = [sel_t[e:e + 1, :] for e in range(G * E)]
    sco = [scores_t[e:e + 1, :] for e in range(G * E)]
    gs = []
    for g in range(G):
        a, b, c, d = sel[E * g:E * g + E]
        gs.append(jnp.maximum(jnp.maximum(jnp.maximum(a + b, a + c), jnp.maximum(a + d, b + c)),
                              jnp.maximum(b + d, c + d)))
    gmax = jnp.maximum(jnp.maximum(gs[0], gs[1]), jnp.maximum(gs[2], gs[3]))
    grp = jnp.where(gs[0] == gmax, 0, jnp.where(gs[1] == gmax, 1, jnp.where(gs[2] == gmax, 2, 3)))

    def pick(rows, j):
        return jnp.where(grp == 0, rows[j], jnp.where(grp == 1, rows[E + j],
                                                      jnp.where(grp == 2, rows[2 * E + j], rows[3 * E + j])))

    ig = [pick(sel, j) for j in range(E)]
    igs = [pick(sco, j) for j in range(E)]

    def first_argmax(v):
        mx = jnp.maximum(jnp.maximum(v[0], v[1]), jnp.maximum(v[2], v[3]))
        return jnp.where(v[0] == mx, 0, jnp.where(v[1] == mx, 1, jnp.where(v[2] == mx, 2, 3)))

    l1 = first_argmax(ig)
    ig2 = [jnp.where(l1 == j, -jnp.inf, ig[j]) for j in range(E)]
    l2 = first_argmax(ig2)

    def take(v, idx):
        return jnp.where(idx == 0, v[0], jnp.where(idx == 1, v[1], jnp.where(idx == 2, v[2], v[3])))

    w1 = take(igs, l1)
    w2 = take(igs, l2)
    den = w1 + w2
    w1 = w1 / den
    w2 = w2 / den
    out = []
    for g in range(G):
        for j in range(E):
            cj = jnp.where(l1 == j, w1, jnp.where(l2 == j, w2, 0.0))
            out.append(jnp.where(grp == g, cj, 0.0))
    return out


def _k5_kernel(x_ref, mod_ref, ng1_ref, ng2_ref, oa_ref, ob_ref, oc_ref, od_ref,
               wg_ref, wb_ref, wo_ref, wr_ref, rb_ref, x1_ref, h2_ref, cmb_ref):
    x = x_ref[0]
    D = x.shape[1]
    tm = x.shape[0]
    sh, sc, g_m = mod_ref[0, 0:1, :], mod_ref[0, 1:2, :], mod_ref[0, 2:3, :]
    sh_f, sc_f = mod_ref[0, 3:4, :], mod_ref[0, 4:5, :]
    hb = (_rms(x, ng1_ref[...]) * (1.0 + sc) + sh).astype(BF16)
    merged = None
    for b, o_ref in enumerate((oa_ref, ob_ref, oc_ref, od_ref)):
        gate = _sigmoid(_dot(hb, wg_ref[:, b * D:(b + 1) * D]))
        t = gate * _dot(o_ref[0], wb_ref[b])
        merged = t if merged is None else merged + t
    x1 = x + g_m * _dot(merged.astype(BF16), wo_ref[...])
    x1_ref[0] = x1
    h2 = (_rms(x1, ng2_ref[...]) * (1.0 + sc_f) + sh_f).astype(BF16)
    h2_ref[0] = h2
    scores = _sigmoid(_dot(h2, wr_ref[...]))
    st = scores.T[0:N_EXPERTS, :]
    rows = _route(st, st + rb_ref[...])
    nr = N_EXPERTS + 8
    rid = lax.broadcasted_iota(jnp.int32, (nr, tm), 0)
    top = jnp.where(rid == N_EXPERTS, 1.0, 0.0)
    for e in range(N_EXPERTS):
        top = jnp.where(rid == e, rows[e], top)
    cmb_ref[0] = jnp.concatenate([top, jnp.zeros((LANES - nr, tm), F32)], axis=0).T


def _k5(x, mod, ng1, ng2, oa, ob, oc, od, lw, wr, rb, tm):
    B, S, D = x.shape
    bs = lambda w: pl.BlockSpec((1, tm, w), lambda b, i: (b, i, 0))
    full = lambda a: pl.BlockSpec(a.shape, lambda b, i: (0,) * a.ndim)
    consts = (lw["wg"], lw["wb"], lw["wo"], wr, rb)
    return pl.pallas_call(
        _k5_kernel,
        out_shape=(jax.ShapeDtypeStruct((B, S, D), F32), jax.ShapeDtypeStruct((B, S, D), BF16),
                   jax.ShapeDtypeStruct((B, S, LANES), F32)),
        grid=(B, S // tm),
        in_specs=[bs(D), pl.BlockSpec((1, 6, D), lambda b, i: (b, 0, 0)), full(ng1), full(ng2),
                  bs(MIX_W), bs(MIX_W), bs(MIX_W), bs(MIX_W)] + [full(a) for a in consts],
        out_specs=(bs(D), bs(D), bs(LANES)),
        name="merge_route",
        compiler_params=_cparams(("arbitrary", "arbitrary")),
    )(x, mod, ng1, ng2, oa, ob, oc, od, *consts)


def _moe_kernel(h2_ref, cmb_ref, x1_ref, mod_ref, wgu_ref, wd_ref, out_ref, acc_ref):
    e = pl.program_id(1)

    @pl.when(e == 0)
    def _():
        acc_ref[...] = jnp.zeros_like(acc_ref)

    gu = _dot(h2_ref[...], wgu_ref[0])
    g = gu[:, :D_EXPERT]
    u = gu[:, D_EXPERT:]
    cmb = cmb_ref[...]
    lane = lax.broadcasted_iota(jnp.int32, cmb.shape, 1)
    c = jnp.sum(jnp.where(lane == e, cmb, 0.0), axis=-1, keepdims=True)
    a = g * _sigmoid(g) * u * c
    acc_ref[...] += _dot(a.astype(BF16), wd_ref[0])

    @pl.when(e == pl.num_programs(1) - 1)
    def _():
        out_ref[...] = x1_ref[...] + mod_ref[0, 5:6, :] * acc_ref[...]


def _moe(h2, cmb, x1, mod, wgu, wd, S, tm):
    N, D = h2.shape
    NE = wgu.shape[0]
    per_b = S // tm
    row = lambda w: pl.BlockSpec((tm, w), lambda i, e: (i, 0))
    return pl.pallas_call(
        _moe_kernel,
        out_shape=jax.ShapeDtypeStruct((N, D), F32),
        grid=(N // tm, NE),
        in_specs=[row(D), row(LANES), row(D),
                  pl.BlockSpec((1, 6, D), lambda i, e: (i // per_b, 0, 0)),
                  pl.BlockSpec((1, D, 2 * D_EXPERT), lambda i, e: (e, 0, 0)),
                  pl.BlockSpec((1, D_EXPERT, D), lambda i, e: (e, 0, 0))],
        out_specs=row(D),
        scratch_shapes=[pltpu.VMEM((tm, D), F32)],
        name="moe",
        compiler_params=_cparams(("arbitrary", "arbitrary")),
    )(h2, cmb, x1, mod, wgu, wd)


def _const_tables():
    lane = np.arange(MIX_W)
    bd = (lane[:, None] // HEAD_DIM == lane[None, :] // HEAD_DIM).astype(np.float32)
    mp = np.zeros((2, MIX_W, MIX_W), np.float32)
    hm_c = np.zeros((2, 2, MIX_W), np.float32)
    for p in range(2):
        head = np.where(lane < LANES, 2 * p + lane // HEAD_DIM, ((lane - LANES) % HEAD_DIM) // C_HALF)
        mp[p] = head[:, None] == head[None, :]
        for a in range(2):
            hm_c[p, a] = head == 2 * p + a
    hm = np.zeros((2, 2, PAIR_W), np.float32)
    for a in range(2):
        hm[:, a] = (np.arange(PAIR_W) // HEAD_DIM == a)
    T = ATT_BLOCK
    j = np.arange(T)
    ustrict = (j[:, None] > j[None, :]).astype(np.float32)
    uo = np.concatenate([ustrict, np.ones((T, T), np.float32)], axis=1)
    uo = np.concatenate([uo, uo], axis=0)
    return (jnp.asarray(bd, BF16), jnp.asarray(mp, BF16), jnp.asarray(hm, BF16),
            jnp.asarray(hm_c, BF16), jnp.asarray(uo, BF16))


def _band_bias(rel_bias):
    T = ATT_BLOCK
    nkb = D_LEFT_CHUNKS * CHUNK // T
    nb = nkb + 1
    jb = np.arange(nb)[:, None, None]
    r = np.arange(T)[None, :, None]
    c = np.arange(T)[None, None, :]
    dist = r - c + (nkb - jb) * T
    idx = np.clip(dist, -D_MAX_REL, D_MAX_REL) + D_MAX_REL
    kc = (jb - nkb) * (T // CHUNK) + c // CHUNK
    rc = r // CHUNK
    vis = (kc <= rc) & (kc >= rc - D_LEFT_CHUNKS)
    bias = rel_bias[:, :, jnp.asarray(idx)]
    return jnp.where(jnp.asarray(vis)[None, None], bias, NEG_INF).astype(F32)


def _pack_weights(w_in, fox_forget_b, fox_q_g, fox_k_g, mla_cq_g, mla_ckv_g, mla_w_uq, mla_w_ukv,
                  mla_q_g, mla_k_g, chunk_q_g, chunk_k_g, w_branch, w_out):
    L, D, _ = w_in.shape
    W = MIX_W
    a0, b0 = 0, 3 * W + N_HEADS
    c0 = b0 + 3 * W
    d0 = c0 + C_Q_RANK + C_KV_RANK + C_ROPE
    g0 = d0 + 3 * W
    scale = HEAD_DIM ** -0.5
    kr = w_in[:, :, c0 + C_Q_RANK + C_KV_RANK:d0]
    w1 = jnp.concatenate([
        w_in[:, :, a0:a0 + 3 * W],
        w_in[:, :, b0:b0 + W] * scale, w_in[:, :, b0 + W:b0 + 3 * W],
        w_in[:, :, d0:d0 + 3 * W],
        w_in[:, :, c0:c0 + C_Q_RANK + C_KV_RANK],
        jnp.tile(kr[:, :, :C_HALF], (1, 1, N_HEADS)), jnp.tile(kr[:, :, C_HALF:], (1, 1, N_HEADS)),
    ], axis=2).astype(BF16)
    wfa = jnp.pad(jnp.swapaxes(w_in[:, :, 3 * W:3 * W + N_HEADS], 1, 2), ((0, 0), (0, 8 - N_HEADS), (0, 0))).astype(BF16)
    fb = jnp.pad(fox_forget_b, ((0, 0), (0, 8 - N_HEADS))).reshape(L, 8, 1).astype(F32)
    wg = w_in[:, :, g0:].astype(BF16)

    uq = mla_w_uq.reshape(L, C_Q_RANK, N_HEADS, C_QK)
    uq_rope = jnp.concatenate([uq[..., C_NOPE:C_NOPE + C_HALF].reshape(L, C_Q_RANK, N_HEADS * C_HALF),
                               uq[..., C_NOPE + C_HALF:].reshape(L, C_Q_RANK, N_HEADS * C_HALF)], axis=2)
    wuq = jnp.concatenate([uq[:, :, 0, :C_NOPE], uq[:, :, 1, :C_NOPE], uq_rope,
                           uq[:, :, 2, :C_NOPE], uq[:, :, 3, :C_NOPE], uq_rope], axis=2).astype(BF16)
    ukv = mla_w_ukv.reshape(L, C_KV_RANK, N_HEADS, C_NOPE + HEAD_DIM)
    wukv = jnp.concatenate([ukv[..., :C_NOPE].reshape(L, C_KV_RANK, W),
                            ukv[..., C_NOPE:].reshape(L, C_KV_RANK, W)], axis=2).astype(BF16)

    def c_gain(g, s):
        nope = g[:, :C_NOPE]
        rope = jnp.concatenate([jnp.tile(g[:, C_NOPE:C_NOPE + C_HALF], (1, N_HEADS)),
                                jnp.tile(g[:, C_NOPE + C_HALF:], (1, N_HEADS))], axis=1)
        pair = jnp.concatenate([nope, nope, rope], axis=1)
        return jnp.concatenate([pair, pair], axis=1) * s

    def pad512(v):
        return jnp.pad(v, ((0, 0), (0, 2 * W - v.shape[1])))

    gv = jnp.stack([
        pad512(jnp.tile(fox_q_g, (1, N_HEADS)) * scale), pad512(jnp.tile(fox_k_g, (1, N_HEADS))),
        pad512(jnp.tile(chunk_q_g, (1, N_HEADS)) * scale), pad512(jnp.tile(chunk_k_g, (1, N_HEADS))),
        pad512(mla_cq_g), pad512(mla_ckv_g), c_gain(mla_q_g, C_QK ** -0.5), c_gain(mla_k_g, 1.0),
    ], axis=1).astype(F32)
    return dict(w1=w1, wfa=wfa, fb=fb, wg=wg, wuq=wuq, wukv=wukv, gv=gv,
                wb=w_branch.astype(BF16), wo=w_out.astype(BF16))


def kernel(x, c, positions, norm_mix_g, norm_ffn_g, w_ada, b_ada, w_in, fox_forget_b, fox_q_g, fox_k_g, mla_cq_g, mla_ckv_g, mla_w_uq, mla_w_ukv, mla_q_g, mla_k_g, chunk_q_g, chunk_k_g, chunk_rel_bias, w_branch, w_out, router_w, router_b, exp_w_gate, exp_w_up, exp_w_down, sh_w_gate, sh_w_up, sh_w_down):
    B, S, D = x.shape
    L = w_in.shape[0]
    assert S % ATT_BLOCK == 0 and D % LANES == 0
    tm_p1 = min(S, 512)
    tm_k5 = min(S, 256)
    tm_moe = min(S, 1024)

    bd, mp, hm, hm_c, uo = _const_tables()
    pw = _pack_weights(w_in, fox_forget_b, fox_q_g, fox_k_g, mla_cq_g, mla_ckv_g, mla_w_uq, mla_w_ukv,
                       mla_q_g, mla_k_g, chunk_q_g, chunk_k_g, w_branch, w_out)
    band_bias = _band_bias(chunk_rel_bias)
    wr = jnp.pad(router_w, ((0, 0), (0, LANES - N_EXPERTS))).astype(BF16)
    rb = router_b.reshape(N_EXPERTS, 1).astype(F32)
    wgu = jnp.concatenate([jnp.concatenate([exp_w_gate, exp_w_up], axis=-1),
                           jnp.concatenate([sh_w_gate, sh_w_up], axis=-1)[:, None]], axis=1).astype(BF16)
    wd = jnp.concatenate([exp_w_down, sh_w_down[:, None]], axis=1).astype(BF16)

    mod = _modulation(c, w_ada, b_ada).reshape(L, B, 6, D)
    cosl, sinl = _rope_tables(positions)

    for l in range(L):
        lw = {k: v[l] for k, v in pw.items()}
        lw["bd"], lw["mp"] = bd, mp
        ng1 = norm_mix_g[l].reshape(1, D)
        ng2 = norm_ffn_g[l].reshape(1, D)
        (qa, ka, va, lf, qb, kb, vb, qc, kc, vc, qd, kd, vd) = _p1(x, mod[l], ng1, lw, cosl, sinl, tm_p1)
        oa = _softmax_attn("fox", qa, ka, va, hm, lf)
        ob = _stick_attn(qb, kb, vb, hm, uo)
        oc = _softmax_attn("mla", qc, kc, vc, hm_c)
        od = _softmax_attn("band", qd, kd, vd, hm, band_bias[l])
        x1, h2, cmb = _k5(x, mod[l], ng1, ng2, oa, ob, oc, od, lw, wr, rb, tm_k5)
        x = _moe(h2.reshape(B * S, D), cmb.reshape(B * S, LANES), x1.reshape(B * S, D), mod[l],
                 wgu[l], wd[l], S, tm_moe).reshape(B, S, D)
    return x
```

```python
import functools
import math

import numpy as np
import jax
import jax.numpy as jnp
from jax import lax
from jax.experimental import pallas as pl
from jax.experimental.pallas import tpu as pltpu

F32 = jnp.float32
BF16 = jnp.bfloat16

HEAD_DIM = 64
N_HEADS = 4
MIX_W = N_HEADS * HEAD_DIM
CHUNK = 64
NEG_INF = -1e30
EPS = 1e-6
C_Q_RANK = 256
C_KV_RANK = 128
C_NOPE = 64
C_ROPE = 32
C_HALF = C_ROPE // 2
C_QK = C_NOPE + C_ROPE
ROPE_BASE = 10000.0
D_LEFT_CHUNKS = 8
D_MAX_REL = 128
N_BRANCH = 4
N_EXPERTS = 16
N_GROUPS = 4
EXPERTS_PER_GROUP = 4
D_EXPERT = 256

LANES = 128
ATT_BLOCK = 512
BAND_TQ = 256
BAND_LEFT = D_LEFT_CHUNKS * CHUNK
BAND_WIN = BAND_LEFT + BAND_TQ
PAIR_W = 2 * HEAD_DIM
VMEM_LIMIT = 56 * 1024 * 1024

P1_COLS = 9 * MIX_W + C_Q_RANK + C_KV_RANK + LANES


def _cparams(sem):
    return pltpu.CompilerParams(dimension_semantics=sem, vmem_limit_bytes=VMEM_LIMIT)


def _dot(a, b):
    return jnp.dot(a, b, preferred_element_type=F32)


def _dot_nt(a, b):
    return lax.dot_general(a, b, (((1,), (1,)), ((), ())), preferred_element_type=F32)


def _sigmoid(x):
    return 1.0 / (1.0 + jnp.exp(-x))


def _rms(x, g):
    return x * lax.rsqrt(jnp.mean(x * x, axis=-1, keepdims=True) + EPS) * g


def _mod_kernel(c_ref, w_ref, b_ref, o_ref):
    c = c_ref[...]
    cond = c * _sigmoid(c)
    o_ref[0] = jnp.dot(cond, w_ref[0], preferred_element_type=F32,
                       precision=lax.Precision.HIGHEST) + b_ref[0]


def _modulation(c, w_ada, b_ada):
    L, D, D6 = w_ada.shape
    B = c.shape[0]
    tn = 1024
    return pl.pallas_call(
        _mod_kernel,
        out_shape=jax.ShapeDtypeStruct((L, B, D6), F32),
        grid=(L, D6 // tn),
        in_specs=[pl.BlockSpec((B, D), lambda l, j: (0, 0)),
                  pl.BlockSpec((1, D, tn), lambda l, j: (l, 0, j)),
                  pl.BlockSpec((1, 1, tn), lambda l, j: (l, 0, j))],
        out_specs=pl.BlockSpec((1, B, tn), lambda l, j: (l, 0, j)),
        name="adaln_mod",
        compiler_params=_cparams(("arbitrary", "arbitrary")),
    )(c, w_ada, b_ada.reshape(L, 1, D6))


def _rope_kernel(ang_ref, sign_ref, cos_ref, sin_ref):
    a = ang_ref[0]
    cos_ref[0] = jnp.cos(a)
    sin_ref[0] = jnp.sin(a) * sign_ref[...]


def _rope_tables(positions):
    B, S = positions.shape
    inv_freq = jnp.power(ROPE_BASE, -jnp.arange(C_HALF, dtype=F32) / C_HALF)
    inv_l = jnp.tile(inv_freq, LANES // C_HALF)
    ang = positions.astype(F32)[:, :, None] * inv_l[None, None, :]
    sign = jnp.where(jnp.arange(LANES) < LANES // 2, -1.0, 1.0).astype(F32).reshape(1, LANES)
    ts = min(S, 1024)
    spec = pl.BlockSpec((1, ts, LANES), lambda b, i: (b, i, 0))
    return pl.pallas_call(
        _rope_kernel,
        out_shape=(jax.ShapeDtypeStruct((B, S, LANES), F32),) * 2,
        grid=(B, S // ts),
        in_specs=[spec, pl.BlockSpec((1, LANES), lambda b, i: (0, 0))],
        out_specs=(spec, spec),
        name="rope_tables",
        compiler_params=_cparams(("arbitrary", "arbitrary")),
    )(ang, sign)


def _rope(t, cos, sin):
    return t * cos + pltpu.roll(t, LANES // 2, 1) * sin


def _p1_kernel(x_ref, mod_ref, ng_ref, w1_ref, wfa_ref, fb_ref, gv_ref, wuq_ref, wukv_ref,
               bd_ref, mp_ref, cos_ref, sin_ref,
               qa_ref, ka_ref, va_ref, lf_ref, qb_ref, kb_ref, vb_ref,
               qc_ref, kc_ref, vc_ref, qd_ref, kd_ref, vd_ref):
    x = x_ref[0]
    sh = mod_ref[0, 0:1, :]
    sc = mod_ref[0, 1:2, :]
    hb = (_rms(x, ng_ref[...]) * (1.0 + sc) + sh).astype(BF16)
    W = MIX_W

    def mm(c0, c1):
        return _dot(hb, w1_ref[:, c0:c1])

    def head_norm(t, g):
        ss = _dot((t * t).astype(BF16), bd_ref[...])
        return (t * lax.rsqrt(ss * (1.0 / HEAD_DIM) + EPS) * g).astype(BF16)

    r = mm(0, 3 * W)
    qa_ref[0] = head_norm(r[:, 0:W], gv_ref[0:1, 0:W])
    ka_ref[0] = head_norm(r[:, W:2 * W], gv_ref[1:2, 0:W])
    va_ref[0] = r[:, 2 * W:3 * W].astype(BF16)
    fa = _dot_nt(wfa_ref[...], hb) + fb_ref[...]
    lf_ref[0] = jnp.minimum(fa, 0.0) - jnp.log(1.0 + jnp.exp(-jnp.abs(fa)))

    r = mm(3 * W, 6 * W)
    qb_ref[0] = r[:, 0:W].astype(BF16)
    kb_ref[0] = r[:, W:2 * W].astype(BF16)
    vb_ref[0] = r[:, 2 * W:3 * W].astype(BF16)

    r = mm(6 * W, 9 * W)
    qd_ref[0] = head_norm(r[:, 0:W], gv_ref[2:3, 0:W])
    kd_ref[0] = head_norm(r[:, W:2 * W], gv_ref[3:4, 0:W])
    vd_ref[0] = r[:, 2 * W:3 * W].astype(BF16)

    c0 = 9 * W
    r = mm(c0, c0 + C_Q_RANK + C_KV_RANK + LANES)
    cos = cos_ref[0]
    sin = sin_ref[0]
    cq = _rms(r[:, 0:C_Q_RANK], gv_ref[4:5, 0:C_Q_RANK]).astype(BF16)
    ckv = _rms(r[:, C_Q_RANK:C_Q_RANK + C_KV_RANK], gv_ref[5:6, 0:C_KV_RANK]).astype(BF16)
    kr = _rope(r[:, C_Q_RANK + C_KV_RANK:], cos, sin)
    qu = _dot(cq, wuq_ref[...])
    kvu = _dot(ckv, wukv_ref[...])
    vc_ref[0] = kvu[:, 2 * LANES:].astype(BF16)
    for p in range(2):
        qp = jnp.concatenate([qu[:, p * 2 * LANES:p * 2 * LANES + LANES],
                              _rope(qu[:, p * 2 * LANES + LANES:(p + 1) * 2 * LANES], cos, sin)], axis=1)
        kp = jnp.concatenate([kvu[:, p * LANES:(p + 1) * LANES], kr], axis=1)
        for t, gr, dst in ((qp, 6, qc_ref), (kp, 7, kc_ref)):
            ss = _dot((t * t).astype(BF16), mp_ref[p])
            tn = t * lax.rsqrt(ss * (1.0 / C_QK) + EPS) * gv_ref[gr:gr + 1, p * 2 * LANES:(p + 1) * 2 * LANES]
            dst[0, :, p * 2 * LANES:(p + 1) * 2 * LANES] = tn.astype(BF16)


def _p1(x, mod, ng, lw, cosl, sinl, tm):
    B, S, D = x.shape
    W = MIX_W
    bs = lambda w: pl.BlockSpec((1, tm, w), lambda b, i: (b, i, 0))
    full = lambda a: pl.BlockSpec(a.shape, lambda b, i: (0,) * a.ndim)
    o256 = jax.ShapeDtypeStruct((B, S, W), BF16)
    o512 = jax.ShapeDtypeStruct((B, S, 2 * W), BF16)
    out_shape = (o256, o256, o256, jax.ShapeDtypeStruct((B, 8, S), F32),
                 o256, o256, o256, o512, o512, o256, o256, o256, o256)
    out_specs = (bs(W), bs(W), bs(W), pl.BlockSpec((1, 8, tm), lambda b, i: (b, 0, i)),
                 bs(W), bs(W), bs(W), bs(2 * W), bs(2 * W), bs(W), bs(W), bs(W), bs(W))
    consts = (ng, lw["w1"], lw["wfa"], lw["fb"], lw["gv"], lw["wuq"], lw["wukv"], lw["bd"], lw["mp"])
    return pl.pallas_call(
        _p1_kernel,
        out_shape=out_shape,
        grid=(B, S // tm),
        in_specs=[bs(D), pl.BlockSpec((1, 6, D), lambda b, i: (b, 0, 0))]
                 + [full(a) for a in consts] + [bs(LANES), bs(LANES)],
        out_specs=out_specs,
        name="in_proj",
        compiler_params=_cparams(("arbitrary", "arbitrary")),
    )(x, mod, *consts, cosl, sinl)


def _lane_cumsum(x):
    n = x.shape[1]
    idx = lax.broadcasted_iota(jnp.int32, x.shape, 1)
    k = 1
    while k < n:
        x = x + jnp.where(idx >= k, pltpu.roll(x, k, 1), 0.0)
        k *= 2
    return x


def _softmax_attn_kernel(mode, S, *refs):
    T = ATT_BLOCK
    if mode == "fox":
        q_ref, k_ref, v_ref, hm_ref, lf_ref, o_ref, cf_ref = refs
        cf = _lane_cumsum(lf_ref[0])
        for h in range(N_HEADS):
            cf_ref[h] = cf[h:h + 1, :]
    else:
        q_ref, k_ref, v_ref, hm_ref, o_ref = refs
    pair = pl.program_id(1)
    row = lax.broadcasted_iota(jnp.int32, (T, T), 0)
    col = lax.broadcasted_iota(jnp.int32, (T, T), 1)
    if mode == "fox":
        dmask = col <= row
    else:
        dmask = (col // CHUNK) <= (row // CHUNK)
    lane = lax.broadcasted_iota(jnp.int32, (T, PAIR_W), 1)

    def q_block(qi, _):
        q0 = pl.multiple_of(qi * T, T)
        qp = q_ref[0, pl.ds(q0, T), :]
        qm = [qp * hm_ref[0, a:a + 1, :] for a in range(2)]

        def step(kj, carry, diag):
            k0 = pl.multiple_of(kj * T, T)
            kt = k_ref[0, pl.ds(k0, T), :]
            vt = v_ref[0, pl.ds(k0, T), :]
            new = []
            for a in range(2):
                m, l, acc = carry[a]
                s = _dot_nt(qm[a], kt)
                if mode == "fox":
                    s = s - cf_ref[2 * pair + a, :, pl.ds(k0, T)]
                if diag:
                    s = jnp.where(dmask, s, NEG_INF)
                m_new = jnp.maximum(m, jnp.max(s, axis=-1, keepdims=True))
                alpha = jnp.exp(m - m_new)
                pe = jnp.exp(s - m_new)
                l = alpha * l + jnp.sum(pe, axis=-1, keepdims=True)
                acc = alpha * acc + _dot(pe.astype(BF16), vt)
                new.append((m_new, l, acc))
            return tuple(new)

        init = tuple((jnp.full((T, 1), NEG_INF, F32), jnp.zeros((T, 1), F32),
                      jnp.zeros((T, PAIR_W), F32)) for _ in range(2))
        carry = lax.fori_loop(0, qi, lambda kj, c: step(kj, c, False), init)
        carry = step(qi, carry, True)
        (_, l0, a0), (_, l1, a1) = carry
        o = jnp.where(lane < HEAD_DIM, a0 / l0, a1 / l1)
        o_ref[0, pl.ds(q0, T), :] = o.astype(o_ref.dtype)
        return 0

    lax.fori_loop(0, S // T, q_block, 0)


def _softmax_attn(mode, q, k, v, hm, lf=None):
    B, S, _ = v.shape
    cw = q.shape[2] // 2
    qspec = pl.BlockSpec((1, S, cw), lambda b, p: (b, 0, p))
    vspec = pl.BlockSpec((1, S, PAIR_W), lambda b, p: (b, 0, p))
    in_specs = [qspec, qspec, vspec, pl.BlockSpec((1, 2, cw), lambda b, p: (p, 0, 0))]
    args = [q, k, v, hm]
    scratch = []
    if mode == "fox":
        in_specs.append(pl.BlockSpec((1, 8, S), lambda b, p: (b, 0, 0)))
        args.append(lf)
        scratch = [pltpu.VMEM((N_HEADS, 1, S), F32)]
    return pl.pallas_call(
        functools.partial(_softmax_attn_kernel, mode, S),
        out_shape=jax.ShapeDtypeStruct((B, S, MIX_W), BF16),
        grid=(B, 2),
        in_specs=in_specs,
        out_specs=vspec,
        scratch_shapes=scratch,
        name="attn_" + mode,
        compiler_params=_cparams(("arbitrary", "arbitrary")),
    )(*args)


def _band_kernel(S, q_ref, k_ref, v_ref, hm_ref, bias_ref, o_ref):
    lane = lax.broadcasted_iota(jnp.int32, (BAND_TQ, PAIR_W), 1)

    def q_block(qi, _):
        q0 = pl.multiple_of(qi * BAND_TQ, BAND_TQ)
        k0 = pl.multiple_of(jnp.maximum(q0 - BAND_LEFT, 0), BAND_TQ)
        case = jnp.minimum(qi, BAND_LEFT // BAND_TQ)
        qp = q_ref[0, pl.ds(q0, BAND_TQ), :]
        kt = k_ref[0, pl.ds(k0, BAND_WIN), :]
        vt = v_ref[0, pl.ds(k0, BAND_WIN), :]
        outs = []
        for a in range(2):
            s = _dot_nt(qp * hm_ref[0, a:a + 1, :], kt) + bias_ref[a, case]
            m = jnp.max(s, axis=-1, keepdims=True)
            pe = jnp.exp(s - m)
            l = jnp.sum(pe, axis=-1, keepdims=True)
            outs.append(_dot(pe.astype(BF16), vt) / l)
        o = jnp.where(lane < HEAD_DIM, outs[0], outs[1])
        o_ref[0, pl.ds(q0, BAND_TQ), :] = o.astype(o_ref.dtype)
        return 0

    lax.fori_loop(0, S // BAND_TQ, q_block, 0)


def _band_attn(q, k, v, hm, bias):
    B, S, _ = v.shape
    spec = pl.BlockSpec((1, S, PAIR_W), lambda p, b: (b, 0, p))
    return pl.pallas_call(
        functools.partial(_band_kernel, S),
        out_shape=jax.ShapeDtypeStruct((B, S, MIX_W), BF16),
        grid=(2, B),
        in_specs=[spec, spec, spec, pl.BlockSpec((1, 2, PAIR_W), lambda p, b: (p, 0, 0)),
                  pl.BlockSpec((2,) + bias.shape[1:], lambda p, b: (p, 0, 0, 0))],
        out_specs=spec,
        name="attn_band",
        compiler_params=_cparams(("arbitrary", "arbitrary")),
    )(q, k, v, hm, bias)


def _stick_kernel(S, q_ref, k_ref, v_ref, hm_ref, uo_ref, o_ref):
    T = ATT_BLOCK
    row = lax.broadcasted_iota(jnp.int32, (T, T), 0)
    col = lax.broadcasted_iota(jnp.int32, (T, T), 1)
    strict = col < row
    lane = lax.broadcasted_iota(jnp.int32, (T, PAIR_W), 1)

    def q_block(qi, _):
        q0 = pl.multiple_of(qi * T, T)
        qp = q_ref[0, pl.ds(q0, T), :]
        qm = [qp * hm_ref[0, a:a + 1, :] for a in range(2)]

        def step(kj, carry, diag):
            k0 = pl.multiple_of(kj * T, T)
            kt = k_ref[0, pl.ds(k0, T), :]
            vt = v_ref[0, pl.ds(k0, T), :]
            new = []
            for a in range(2):
                acc, cs = carry[a]
                z = _dot_nt(qm[a], kt)
                sp = jnp.maximum(z, 0.0) + jnp.log(1.0 + jnp.exp(-jnp.abs(z)))
                if diag:
                    sp = jnp.where(strict, sp, 0.0)
                hi = sp.astype(BF16)
                lo = (sp - hi.astype(F32)).astype(BF16)
                within, total = [], []
                for j in range(T // LANES):
                    sl = slice(j * LANES, (j + 1) * LANES)
                    cum = _dot(jnp.concatenate([hi[:, sl], lo[:, sl]], axis=1), uo_ref[...])
                    within.append(cum[:, :LANES])
                    total.append(cum[:, LANES:])
                right = [None] * len(total)
                run = cs
                for j in reversed(range(len(total))):
                    right[j] = run
                    run = run + total[j]
                w = jnp.exp(z - sp - jnp.concatenate(within, axis=1) - jnp.concatenate(right, axis=1))
                if diag:
                    w = jnp.where(strict, w, 0.0)
                acc = acc + _dot(w.astype(BF16), vt)
                new.append((acc, run))
            return tuple(new)

        init = tuple((jnp.zeros((T, PAIR_W), F32), jnp.zeros((T, LANES), F32)) for _ in range(2))
        carry = step(qi, init, True)
        carry = lax.fori_loop(0, qi, lambda i, c: step(qi - 1 - i, c, False), carry)
        (a0, _), (a1, _) = carry
        o = jnp.where(lane < HEAD_DIM, a0, a1)
        o_ref[0, pl.ds(q0, T), :] = o.astype(o_ref.dtype)
        return 0

    lax.fori_loop(0, S // T, q_block, 0)


def _stick_attn(q, k, v, hm, uo):
    B, S, _ = v.shape
    spec = pl.BlockSpec((1, S, PAIR_W), lambda b, p: (b, 0, p))
    return pl.pallas_call(
        functools.partial(_stick_kernel, S),
        out_shape=jax.ShapeDtypeStruct((B, S, MIX_W), BF16),
        grid=(B, 2),
        in_specs=[spec, spec, spec, pl.BlockSpec((1, 2, PAIR_W), lambda b, p: (p, 0, 0)),
                  pl.BlockSpec(uo.shape, lambda b, p: (0, 0))],
        out_specs=spec,
        name="attn_stick",
        compiler_params=_cparams(("arbitrary", "arbitrary")),
    )(q, k, v, hm, uo)


def _route(scores_t, sel_t):
    G, E = N_GROUPS, EXPERTS_PER_GROUP
    sel = [sel_t[e:e + 1, :] for e in range(G * E)]
    sco = [scores_t[e:e + 1, :] for e in range(G * E)]
    gs = []
    for g in range(G):
        a, b, c, d = sel[E * g:E * g + E]
        gs.append(jnp.maximum(jnp.maximum(jnp.maximum(a + b, a + c), jnp.maximum(a + d, b + c)),
                              jnp.maximum(b + d, c + d)))
    gmax = jnp.maximum(jnp.maximum(gs[0], gs[1]), jnp.maximum(gs[2], gs[3]))
    grp = jnp.where(gs[0] == gmax, 0, jnp.where(gs[1] == gmax, 1, jnp.where(gs[2] == gmax, 2, 3)))

    def pick(rows, j):
        return jnp.where(grp == 0, rows[j], jnp.where(grp == 1, rows[E + j],
                                                      jnp.where(grp == 2, rows[2 * E + j], rows[3 * E + j])))

    ig = [pick(sel, j) for j in range(E)]
    igs = [pick(sco, j) for j in range(E)]

    def first_argmax(v):
        mx = jnp.maximum(jnp.maximum(v[0], v[1]), jnp.maximum(v[2], v[3]))
        return jnp.where(v[0] == mx, 0, jnp.where(v[1] == mx, 1, jnp.where(v[2] == mx, 2, 3)))

    l1 = first_argmax(ig)
    ig2 = [jnp.where(l1 == j, -jnp.inf, ig[j]) for j in range(E)]
    l2 = first_argmax(ig2)

    def take(v, idx):
        return jnp.where(idx == 0, v[0], jnp.where(idx == 1, v[1], jnp.where(idx == 2, v[2], v[3])))

    w1 = take(igs, l1)
    w2 = take(igs, l2)
    den = w1 + w2
    w1 = w1 / den
    w2 = w2 / den
    out = []
    for g in range(G):
        for j in range(E):
            cj = jnp.where(l1 == j, w1, jnp.where(l2 == j, w2, 0.0))
            out.append(jnp.where(grp == g, cj, 0.0))
    return out


def _k5_kernel(x_ref, mod_ref, ng1_ref, ng2_ref, oa_ref, ob_ref, oc_ref, od_ref,
               wg_ref, wb_ref, wo_ref, wr_ref, rb_ref, x1_ref, h2_ref, cmb_ref):
    x = x_ref[0]
    D = x.shape[1]
    tm = x.shape[0]
    sh, sc, g_m = mod_ref[0, 0:1, :], mod_ref[0, 1:2, :], mod_ref[0, 2:3, :]
    sh_f, sc_f = mod_ref[0, 3:4, :], mod_ref[0, 4:5, :]
    hb = (_rms(x, ng1_ref[...]) * (1.0 + sc) + sh).astype(BF16)
    merged = None
    for b, o_ref in enumerate((oa_ref, ob_ref, oc_ref, od_ref)):
        gate = _sigmoid(_dot(hb, wg_ref[:, b * D:(b + 1) * D]))
        t = gate * _dot(o_ref[0], wb_ref[b])
        merged = t if merged is None else merged + t
    x1 = x + g_m * _dot(merged.astype(BF16), wo_ref[...])
    x1_ref[0] = x1
    h2 = (_rms(x1, ng2_ref[...]) * (1.0 + sc_f) + sh_f).astype(BF16)
    h2_ref[0] = h2
    scores = _sigmoid(_dot(h2, wr_ref[...]))
    st = scores.T[0:N_EXPERTS, :]
    rows = _route(st, st + rb_ref[...])
    nr = N_EXPERTS + 8
    rid = lax.broadcasted_iota(jnp.int32, (nr, tm), 0)
    top = jnp.where(rid == N_EXPERTS, 1.0, 0.0)
    for e in range(N_EXPERTS):
        top = jnp.where(rid == e, rows[e], top)
    cmb_ref[0] = jnp.concatenate([top, jnp.zeros((LANES - nr, tm), F32)], axis=0).T


def _k5(x, mod, ng1, ng2, oa, ob, oc, od, lw, wr, rb, tm):
    B, S, D = x.shape
    bs = lambda w: pl.BlockSpec((1, tm, w), lambda b, i: (b, i, 0))
    full = lambda a: pl.BlockSpec(a.shape, lambda b, i: (0,) * a.ndim)
    consts = (lw["wg"], lw["wb"], lw["wo"], wr, rb)
    return pl.pallas_call(
        _k5_kernel,
        out_shape=(jax.ShapeDtypeStruct((B, S, D), F32), jax.ShapeDtypeStruct((B, S, D), BF16),
                   jax.ShapeDtypeStruct((B, S, LANES), F32)),
        grid=(B, S // tm),
        in_specs=[bs(D), pl.BlockSpec((1, 6, D), lambda b, i: (b, 0, 0)), full(ng1), full(ng2),
                  bs(MIX_W), bs(MIX_W), bs(MIX_W), bs(MIX_W)] + [full(a) for a in consts],
        out_specs=(bs(D), bs(D), bs(LANES)),
        name="merge_route",
        compiler_params=_cparams(("arbitrary", "arbitrary")),
    )(x, mod, ng1, ng2, oa, ob, oc, od, *consts)


def _moe_kernel(h2_ref, cmb_ref, x1_ref, mod_ref, wgu_ref, wd_ref, out_ref, acc_ref):
    e = pl.program_id(1)

    @pl.when(e == 0)
    def _():
        acc_ref[...] = jnp.zeros_like(acc_ref)

    gu = _dot(h2_ref[...], wgu_ref[0])
    g = gu[:, :D_EXPERT]
    u = gu[:, D_EXPERT:]
    cmb = cmb_ref[...]
    lane = lax.broadcasted_iota(jnp.int32, cmb.shape, 1)
    c = jnp.sum(jnp.where(lane == e, cmb, 0.0), axis=-1, keepdims=True)
    a = g * _sigmoid(g) * u * c
    acc_ref[...] += _dot(a.astype(BF16), wd_ref[0])

    @pl.when(e == pl.num_programs(1) - 1)
    def _():
        out_ref[...] = x1_ref[...] + mod_ref[0, 5:6, :] * acc_ref[...]


def _moe(h2, cmb, x1, mod, wgu, wd, S, tm):
    N, D = h2.shape
    NE = wgu.shape[0]
    per_b = S // tm
    row = lambda w: pl.BlockSpec((tm, w), lambda i, e: (i, 0))
    return pl.pallas_call(
        _moe_kernel,
        out_shape=jax.ShapeDtypeStruct((N, D), F32),
        grid=(N // tm, NE),
        in_specs=[row(D), row(LANES), row(D),
                  pl.BlockSpec((1, 6, D), lambda i, e: (i // per_b, 0, 0)),
                  pl.BlockSpec((1, D, 2 * D_EXPERT), lambda i, e: (e, 0, 0)),
                  pl.BlockSpec((1, D_EXPERT, D), lambda i, e: (e, 0, 0))],
        out_specs=row(D),
        scratch_shapes=[pltpu.VMEM((tm, D), F32)],
        name="moe",
        compiler_params=_cparams(("arbitrary", "arbitrary")),
    )(h2, cmb, x1, mod, wgu, wd)


def _const_tables():
    lane = np.arange(MIX_W)
    bd = (lane[:, None] // HEAD_DIM == lane[None, :] // HEAD_DIM).astype(np.float32)
    mp = np.zeros((2, MIX_W, MIX_W), np.float32)
    hm_c = np.zeros((2, 2, MIX_W), np.float32)
    for p in range(2):
        head = np.where(lane < LANES, 2 * p + lane // HEAD_DIM, ((lane - LANES) % HEAD_DIM) // C_HALF)
        mp[p] = head[:, None] == head[None, :]
        for a in range(2):
            hm_c[p, a] = head == 2 * p + a
    hm = np.zeros((2, 2, PAIR_W), np.float32)
    for a in range(2):
        hm[:, a] = (np.arange(PAIR_W) // HEAD_DIM == a)
    j = np.arange(LANES)
    ustrict = (j[:, None] > j[None, :]).astype(np.float32)
    uo = np.concatenate([ustrict, np.ones((LANES, LANES), np.float32)], axis=1)
    uo = np.concatenate([uo, uo], axis=0)
    return (jnp.asarray(bd, BF16), jnp.asarray(mp, BF16), jnp.asarray(hm, BF16),
            jnp.asarray(hm_c, BF16), jnp.asarray(uo, BF16))


def _band_bias(rel_bias):
    ncase = BAND_LEFT // BAND_TQ + 1
    off = (np.arange(ncase) * BAND_TQ)[:, None, None]
    r = np.arange(BAND_TQ)[None, :, None]
    c = np.arange(BAND_WIN)[None, None, :]
    dist = r + off - c
    idx = np.clip(dist, -D_MAX_REL, D_MAX_REL) + D_MAX_REL
    kc = np.floor_divide(c - off, CHUNK)
    rc = r // CHUNK
    vis = (kc <= rc) & (kc >= rc - D_LEFT_CHUNKS)
    bias = rel_bias[:, :, jnp.asarray(idx)]
    return jnp.where(jnp.asarray(vis)[None, None], bias, NEG_INF).astype(F32)


def _pack_weights(w_in, fox_forget_b, fox_q_g, fox_k_g, mla_cq_g, mla_ckv_g, mla_w_uq, mla_w_ukv,
                  mla_q_g, mla_k_g, chunk_q_g, chunk_k_g, w_branch, w_out):
    L, D, _ = w_in.shape
    W = MIX_W
    a0, b0 = 0, 3 * W + N_HEADS
    c0 = b0 + 3 * W
    d0 = c0 + C_Q_RANK + C_KV_RANK + C_ROPE
    g0 = d0 + 3 * W
    scale = HEAD_DIM ** -0.5
    kr = w_in[:, :, c0 + C_Q_RANK + C_KV_RANK:d0]
    w1 = jnp.concatenate([
        w_in[:, :, a0:a0 + 3 * W],
        w_in[:, :, b0:b0 + W] * scale, w_in[:, :, b0 + W:b0 + 3 * W],
        w_in[:, :, d0:d0 + 3 * W],
        w_in[:, :, c0:c0 + C_Q_RANK + C_KV_RANK],
        jnp.tile(kr[:, :, :C_HALF], (1, 1, N_HEADS)), jnp.tile(kr[:, :, C_HALF:], (1, 1, N_HEADS)),
    ], axis=2).astype(BF16)
    wfa = jnp.pad(jnp.swapaxes(w_in[:, :, 3 * W:3 * W + N_HEADS], 1, 2), ((0, 0), (0, 8 - N_HEADS), (0, 0))).astype(BF16)
    fb = jnp.pad(fox_forget_b, ((0, 0), (0, 8 - N_HEADS))).reshape(L, 8, 1).astype(F32)
    wg = w_in[:, :, g0:].astype(BF16)

    uq = mla_w_uq.reshape(L, C_Q_RANK, N_HEADS, C_QK)
    uq_rope = jnp.concatenate([uq[..., C_NOPE:C_NOPE + C_HALF].reshape(L, C_Q_RANK, N_HEADS * C_HALF),
                               uq[..., C_NOPE + C_HALF:].reshape(L, C_Q_RANK, N_HEADS * C_HALF)], axis=2)
    wuq = jnp.concatenate([uq[:, :, 0, :C_NOPE], uq[:, :, 1, :C_NOPE], uq_rope,
                           uq[:, :, 2, :C_NOPE], uq[:, :, 3, :C_NOPE], uq_rope], axis=2).astype(BF16)
    ukv = mla_w_ukv.reshape(L, C_KV_RANK, N_HEADS, C_NOPE + HEAD_DIM)
    wukv = jnp.concatenate([ukv[..., :C_NOPE].reshape(L, C_KV_RANK, W),
                            ukv[..., C_NOPE:].reshape(L, C_KV_RANK, W)], axis=2).astype(BF16)

    def c_gain(g, s):
        nope = g[:, :C_NOPE]
        rope = jnp.concatenate([jnp.tile(g[:, C_NOPE:C_NOPE + C_HALF], (1, N_HEADS)),
                                jnp.tile(g[:, C_NOPE + C_HALF:], (1, N_HEADS))], axis=1)
        pair = jnp.concatenate([nope, nope, rope], axis=1)
        return jnp.concatenate([pair, pair], axis=1) * s

    def pad512(v):
        return jnp.pad(v, ((0, 0), (0, 2 * W - v.shape[1])))

    gv = jnp.stack([
        pad512(jnp.tile(fox_q_g, (1, N_HEADS)) * scale), pad512(jnp.tile(fox_k_g, (1, N_HEADS))),
        pad512(jnp.tile(chunk_q_g, (1, N_HEADS)) * scale), pad512(jnp.tile(chunk_k_g, (1, N_HEADS))),
        pad512(mla_cq_g), pad512(mla_ckv_g), c_gain(mla_q_g, C_QK ** -0.5), c_gain(mla_k_g, 1.0),
    ], axis=1).astype(F32)
    return dict(w1=w1, wfa=wfa, fb=fb, wg=wg, wuq=wuq, wukv=wukv, gv=gv,
                wb=w_branch.astype(BF16), wo=w_out.astype(BF16))


def kernel(x, c, positions, norm_mix_g, norm_ffn_g, w_ada, b_ada, w_in, fox_forget_b, fox_q_g, fox_k_g, mla_cq_g, mla_ckv_g, mla_w_uq, mla_w_ukv, mla_q_g, mla_k_g, chunk_q_g, chunk_k_g, chunk_rel_bias, w_branch, w_out, router_w, router_b, exp_w_gate, exp_w_up, exp_w_down, sh_w_gate, sh_w_up, sh_w_down):
    B, S, D = x.shape
    L = w_in.shape[0]
    assert S % ATT_BLOCK == 0 and S >= BAND_WIN and D % LANES == 0
    tm_p1 = min(S, 512)
    tm_k5 = min(S, 256)
    tm_moe = min(S, 1024)

    bd, mp, hm, hm_c, uo = _const_tables()
    pw = _pack_weights(w_in, fox_forget_b, fox_q_g, fox_k_g, mla_cq_g, mla_ckv_g, mla_w_uq, mla_w_ukv,
                       mla_q_g, mla_k_g, chunk_q_g, chunk_k_g, w_branch, w_out)
    band_bias = _band_bias(chunk_rel_bias)
    wr = jnp.pad(router_w, ((0, 0), (0, LANES - N_EXPERTS))).astype(BF16)
    rb = router_b.reshape(N_EXPERTS, 1).astype(F32)
    wgu = jnp.concatenate([jnp.concatenate([exp_w_gate, exp_w_up], axis=-1),
                           jnp.concatenate([sh_w_gate, sh_w_up], axis=-1)[:, None]], axis=1).astype(BF16)
    wd = jnp.concatenate([exp_w_down, sh_w_down[:, None]], axis=1).astype(BF16)

    mod = _modulation(c, w_ada, b_ada).reshape(L, B, 6, D)
    cosl, sinl = _rope_tables(positions)

    for l in range(L):
        lw = {k: v[l] for k, v in pw.items()}
        lw["bd"], lw["mp"] = bd, mp
        ng1 = norm_mix_g[l].reshape(1, D)
        ng2 = norm_ffn_g[l].reshape(1, D)
        (qa, ka, va, lf, qb, kb, vb, qc, kc, vc, qd, kd, vd) = _p1(x, mod[l], ng1, lw, cosl, sinl, tm_p1)
        oa = _softmax_attn("fox", qa, ka, va, hm, lf)
        ob = _stick_attn(qb, kb, vb, hm, uo)
        oc = _softmax_attn("mla", qc, kc, vc, hm_c)
        od = _band_attn(qd, kd, vd, hm, band_bias[l])
        x1, h2, cmb = _k5(x, mod[l], ng1, ng2, oa, ob, oc, od, lw, wr, rb, tm_k5)
        x = _moe(h2.reshape(B * S, D), cmb.reshape(B * S, LANES), x1.reshape(B * S, D), mod[l],
                 wgu[l], wd[l], S, tm_moe).reshape(B, S, D)
    return x
```

```python
import functools
import math

import numpy as np
import jax
import jax.numpy as jnp
from jax import lax
from jax.experimental import pallas as pl
from jax.experimental.pallas import tpu as pltpu

F32 = jnp.float32
BF16 = jnp.bfloat16

HEAD_DIM = 64
N_HEADS = 4
MIX_W = N_HEADS * HEAD_DIM
CHUNK = 64
NEG_INF = -1e30
EPS = 1e-6
C_Q_RANK = 256
C_KV_RANK = 128
C_NOPE = 64
C_ROPE = 32
C_HALF = C_ROPE // 2
C_QK = C_NOPE + C_ROPE
ROPE_BASE = 10000.0
D_LEFT_CHUNKS = 8
D_MAX_REL = 128
N_BRANCH = 4
N_EXPERTS = 16
N_GROUPS = 4
EXPERTS_PER_GROUP = 4
D_EXPERT = 256
LOG2E = math.log2(math.e)

LANES = 128
ATT_BLOCK = 512
ROW_BLOCK = 256
SOFTMAX_ROW_BLOCK = 512
MOE_ROW_BLOCK = 256
BAND_TQ = 256
BAND_LEFT = D_LEFT_CHUNKS * CHUNK
BAND_WIN = BAND_LEFT + BAND_TQ
PAIR_W = 2 * HEAD_DIM
VMEM_LIMIT = 56 * 1024 * 1024

P1_COLS = 9 * MIX_W + C_Q_RANK + C_KV_RANK + LANES


def _cparams(sem):
    return pltpu.CompilerParams(dimension_semantics=sem, vmem_limit_bytes=VMEM_LIMIT)


def _dot(a, b):
    return jnp.dot(a, b, preferred_element_type=F32)


def _dot_nt(a, b):
    return lax.dot_general(a, b, (((1,), (1,)), ((), ())), preferred_element_type=F32)


def _sigmoid(x):
    return 1.0 / (1.0 + jnp.exp(-x))


def _rms(x, g):
    return x * lax.rsqrt(jnp.mean(x * x, axis=-1, keepdims=True) + EPS) * g


def _mod_kernel(c_ref, w_ref, b_ref, o_ref):
    c = c_ref[...]
    cond = c * _sigmoid(c)
    o_ref[0] = jnp.dot(cond, w_ref[0], preferred_element_type=F32,
                       precision=lax.Precision.HIGHEST) + b_ref[0]


def _modulation(c, w_ada, b_ada):
    L, D, D6 = w_ada.shape
    B = c.shape[0]
    tn = 1024
    return pl.pallas_call(
        _mod_kernel,
        out_shape=jax.ShapeDtypeStruct((L, B, D6), F32),
        grid=(L, D6 // tn),
        in_specs=[pl.BlockSpec((B, D), lambda l, j: (0, 0)),
                  pl.BlockSpec((1, D, tn), lambda l, j: (l, 0, j)),
                  pl.BlockSpec((1, 1, tn), lambda l, j: (l, 0, j))],
        out_specs=pl.BlockSpec((1, B, tn), lambda l, j: (l, 0, j)),
        name="adaln_mod",
        compiler_params=_cparams(("arbitrary", "arbitrary")),
    )(c, w_ada, b_ada.reshape(L, 1, D6))


def _rope_kernel(ang_ref, sign_ref, cos_ref, sin_ref):
    a = ang_ref[0]
    cos_ref[0] = jnp.cos(a)
    sin_ref[0] = jnp.sin(a) * sign_ref[...]


def _rope_tables(positions):
    B, S = positions.shape
    inv_freq = jnp.power(ROPE_BASE, -jnp.arange(C_HALF, dtype=F32) / C_HALF)
    inv_l = jnp.tile(inv_freq, LANES // C_HALF)
    ang = positions.astype(F32)[:, :, None] * inv_l[None, None, :]
    sign = jnp.where(jnp.arange(LANES) < LANES // 2, -1.0, 1.0).astype(F32).reshape(1, LANES)
    ts = min(S, 1024)
    spec = pl.BlockSpec((1, ts, LANES), lambda b, i: (b, i, 0))
    return pl.pallas_call(
        _rope_kernel,
        out_shape=(jax.ShapeDtypeStruct((B, S, LANES), F32),) * 2,
        grid=(B, S // ts),
        in_specs=[spec, pl.BlockSpec((1, LANES), lambda b, i: (0, 0))],
        out_specs=(spec, spec),
        name="rope_tables",
        compiler_params=_cparams(("arbitrary", "arbitrary")),
    )(ang, sign)


def _rope(t, cos, sin):
    return t * cos + pltpu.roll(t, LANES // 2, 1) * sin


def _p1_kernel(x_ref, mod_ref, ng_ref, w1_ref, wfa_ref, fb_ref, gv_ref, wuq_ref, wukv_ref,
               bd_ref, mp_ref, cos_ref, sin_ref,
               qa_ref, ka_ref, va_ref, lf_ref, qb_ref, kb_ref, vb_ref,
               qc_ref, kc_ref, vc_ref, qd_ref, kd_ref, vd_ref):
    x = x_ref[0]
    sh = mod_ref[0, 0:1, :]
    sc = mod_ref[0, 1:2, :]
    hb = (_rms(x, ng_ref[...]) * (1.0 + sc) + sh).astype(BF16)
    W = MIX_W

    def mm(c0, c1):
        return _dot(hb, w1_ref[:, c0:c1])

    def head_norm(t, g):
        ss = _dot((t * t).astype(BF16), bd_ref[...])
        return (t * lax.rsqrt(ss * (1.0 / HEAD_DIM) + EPS) * g).astype(BF16)

    r = mm(0, 3 * W)
    qa_ref[0] = head_norm(r[:, 0:W], gv_ref[0:1, 0:W])
    ka_ref[0] = head_norm(r[:, W:2 * W], gv_ref[1:2, 0:W])
    va_ref[0] = r[:, 2 * W:3 * W].astype(BF16)
    fa = _dot_nt(wfa_ref[...], hb) + fb_ref[...]
    lf_ref[0] = jnp.minimum(fa, 0.0) - jnp.log(1.0 + jnp.exp(-jnp.abs(fa)))

    r = mm(3 * W, 6 * W)
    qb_ref[0] = r[:, 0:W].astype(BF16)
    kb_ref[0] = r[:, W:2 * W].astype(BF16)
    vb_ref[0] = r[:, 2 * W:3 * W].astype(BF16)

    r = mm(6 * W, 9 * W)
    qd_ref[0] = head_norm(r[:, 0:W], gv_ref[2:3, 0:W])
    kd_ref[0] = head_norm(r[:, W:2 * W], gv_ref[3:4, 0:W])
    vd_ref[0] = r[:, 2 * W:3 * W].astype(BF16)

    c0 = 9 * W
    r = mm(c0, c0 + C_Q_RANK + C_KV_RANK + LANES)
    cos = cos_ref[0]
    sin = sin_ref[0]
    cq = _rms(r[:, 0:C_Q_RANK], gv_ref[4:5, 0:C_Q_RANK]).astype(BF16)
    ckv = _rms(r[:, C_Q_RANK:C_Q_RANK + C_KV_RANK], gv_ref[5:6, 0:C_KV_RANK]).astype(BF16)
    kr = _rope(r[:, C_Q_RANK + C_KV_RANK:], cos, sin)
    qu = _dot(cq, wuq_ref[...])
    kvu = _dot(ckv, wukv_ref[...])
    vc_ref[0] = kvu[:, 2 * LANES:].astype(BF16)
    for p in range(2):
        qp = jnp.concatenate([qu[:, p * 2 * LANES:p * 2 * LANES + LANES],
                              _rope(qu[:, p * 2 * LANES + LANES:(p + 1) * 2 * LANES], cos, sin)], axis=1)
        kp = jnp.concatenate([kvu[:, p * LANES:(p + 1) * LANES], kr], axis=1)
        for t, gr, dst in ((qp, 6, qc_ref), (kp, 7, kc_ref)):
            ss = _dot((t * t).astype(BF16), mp_ref[p])
            tn = t * lax.rsqrt(ss * (1.0 / C_QK) + EPS) * gv_ref[gr:gr + 1, p * 2 * LANES:(p + 1) * 2 * LANES]
            dst[0, :, p * 2 * LANES:(p + 1) * 2 * LANES] = tn.astype(BF16)


def _p1(x, mod, ng, lw, cosl, sinl, tm):
    B, S, D = x.shape
    W = MIX_W
    bs = lambda w: pl.BlockSpec((1, tm, w), lambda b, i: (b, i, 0))
    full = lambda a: pl.BlockSpec(a.shape, lambda b, i: (0,) * a.ndim)
    o256 = jax.ShapeDtypeStruct((B, S, W), BF16)
    o512 = jax.ShapeDtypeStruct((B, S, 2 * W), BF16)
    out_shape = (o256, o256, o256, jax.ShapeDtypeStruct((B, 8, S), F32),
                 o256, o256, o256, o512, o512, o256, o256, o256, o256)
    out_specs = (bs(W), bs(W), bs(W), pl.BlockSpec((1, 8, tm), lambda b, i: (b, 0, i)),
                 bs(W), bs(W), bs(W), bs(2 * W), bs(2 * W), bs(W), bs(W), bs(W), bs(W))
    consts = (ng, lw["w1"], lw["wfa"], lw["fb"], lw["gv"], lw["wuq"], lw["wukv"], lw["bd"], lw["mp"])
    return pl.pallas_call(
        _p1_kernel,
        out_shape=out_shape,
        grid=(B, S // tm),
        in_specs=[bs(D), pl.BlockSpec((1, 6, D), lambda b, i: (b, 0, 0))]
                 + [full(a) for a in consts] + [bs(LANES), bs(LANES)],
        out_specs=out_specs,
        name="in_proj",
        compiler_params=_cparams(("arbitrary", "arbitrary")),
    )(x, mod, *consts, cosl, sinl)


def _lane_cumsum(x):
    n = x.shape[1]
    idx = lax.broadcasted_iota(jnp.int32, x.shape, 1)
    k = 1
    while k < n:
        x = x + jnp.where(idx >= k, pltpu.roll(x, k, 1), 0.0)
        k *= 2
    return x


def _ones_outside(own, v):
    return jnp.where(own[:v.shape[0]], v, jnp.ones_like(v))


def _normalize_pair(acc0, acc1, lane):
    half = PAIR_W // 2
    o0 = acc0 / pltpu.roll(acc0, half, 1)
    o1 = acc1 / pltpu.roll(acc1, half, 1)
    return jnp.where(lane < HEAD_DIM, o0, o1)


def _softmax_attn_kernel(mode, S, *refs):
    T = ATT_BLOCK
    if mode == "fox":
        q_ref, k_ref, v_ref, hm_ref, lf_ref, o_ref, cf_ref = refs
        cf = _lane_cumsum(lf_ref[0]) * LOG2E
        for h in range(N_HEADS):
            cf_ref[h] = cf[h:h + 1, :]
    else:
        q_ref, k_ref, v_ref, hm_ref, o_ref = refs
    pair = pl.program_id(1)
    row = lax.broadcasted_iota(jnp.int32, (T, T), 0)
    col = lax.broadcasted_iota(jnp.int32, (T, T), 1)
    if mode == "fox":
        dmask = col <= row
    else:
        dmask = (col // CHUNK) <= (row // CHUNK)
    lane = lax.broadcasted_iota(jnp.int32, (T, PAIR_W), 1)
    own = [lane // HEAD_DIM == a for a in range(2)]

    def q_block(qi, _):
        q0 = pl.multiple_of(qi * T, T)
        qp = q_ref[0, pl.ds(q0, T), :]
        qm = [qp * hm_ref[0, a:a + 1, :] for a in range(2)]

        def step(kj, carry, diag):
            k0 = pl.multiple_of(kj * T, T)
            kt = k_ref[0, pl.ds(k0, T), :]
            vt = v_ref[0, pl.ds(k0, T), :]
            nrb = T // SOFTMAX_ROW_BLOCK
            chains = [(a, rb) for a in range(2) for rb in range(nrb)]
            rs = [slice(rb * SOFTMAX_ROW_BLOCK, (rb + 1) * SOFTMAX_ROW_BLOCK) for rb in range(nrb)]
            kw = [(rb + 1) * SOFTMAX_ROW_BLOCK if diag else T for rb in range(nrb)]
            s = [_dot_nt(qm[a][rs[rb]], kt[:kw[rb]]) for a, rb in chains]
            if mode == "fox":
                cfk = [cf_ref[2 * pair + a, :, pl.ds(k0, T)] for a in range(2)]
                s = [sc - cfk[a][:, :kw[rb]] for (a, rb), sc in zip(chains, s)]
            if diag:
                s = [jnp.where(dmask[rs[rb], :kw[rb]], sc, NEG_INF) for (a, rb), sc in zip(chains, s)]
            m_new = [jnp.maximum(carry[a][0][rs[rb]], jnp.max(sc, axis=-1, keepdims=True))
                     for (a, rb), sc in zip(chains, s)]
            pe = [jnp.exp2(sc - mn).astype(BF16) for sc, mn in zip(s, m_new)]
            va = [_ones_outside(own[a], vt) for a in range(2)]
            acc = [jnp.exp2(carry[a][0][rs[rb]] - mn) * carry[a][1][rs[rb]] + _dot(p_, va[a][:kw[rb]])
                   for (a, rb), mn, p_ in zip(chains, m_new, pe)]
            return tuple((jnp.concatenate(m_new[a * nrb:(a + 1) * nrb], axis=0),
                          jnp.concatenate(acc[a * nrb:(a + 1) * nrb], axis=0)) for a in range(2))

        init = tuple((jnp.full((T, 1), NEG_INF, F32), jnp.zeros((T, PAIR_W), F32)) for _ in range(2))
        carry = lax.fori_loop(0, qi, lambda kj, c: step(kj, c, False), init)
        carry = step(qi, carry, True)
        o = _normalize_pair(carry[0][1], carry[1][1], lane)
        o_ref[0, pl.ds(q0, T), :] = o.astype(o_ref.dtype)
        return 0

    lax.fori_loop(0, S // T, q_block, 0)


def _softmax_attn(mode, q, k, v, hm, lf=None):
    B, S, _ = v.shape
    cw = q.shape[2] // 2
    qspec = pl.BlockSpec((1, S, cw), lambda b, p: (b, 0, p))
    vspec = pl.BlockSpec((1, S, PAIR_W), lambda b, p: (b, 0, p))
    in_specs = [qspec, qspec, vspec, pl.BlockSpec((1, 2, cw), lambda b, p: (p, 0, 0))]
    args = [q, k, v, hm]
    scratch = []
    if mode == "fox":
        in_specs.append(pl.BlockSpec((1, 8, S), lambda b, p: (b, 0, 0)))
        args.append(lf)
        scratch = [pltpu.VMEM((N_HEADS, 1, S), F32)]
    return pl.pallas_call(
        functools.partial(_softmax_attn_kernel, mode, S),
        out_shape=jax.ShapeDtypeStruct((B, S, MIX_W), BF16),
        grid=(B, 2),
        in_specs=in_specs,
        out_specs=vspec,
        scratch_shapes=scratch,
        name="attn_" + mode,
        compiler_params=_cparams(("arbitrary", "arbitrary")),
    )(*args)


def _band_kernel(S, q_ref, k_ref, v_ref, hm_ref, bias_ref, o_ref):
    lane = lax.broadcasted_iota(jnp.int32, (BAND_TQ, PAIR_W), 1)
    vlane = lax.broadcasted_iota(jnp.int32, (BAND_WIN, PAIR_W), 1)
    own = [vlane // HEAD_DIM == a for a in range(2)]

    def q_block(qi, _):
        q0 = pl.multiple_of(qi * BAND_TQ, BAND_TQ)
        k0 = pl.multiple_of(jnp.maximum(q0 - BAND_LEFT, 0), BAND_TQ)
        case = jnp.minimum(qi, BAND_LEFT // BAND_TQ)
        qp = q_ref[0, pl.ds(q0, BAND_TQ), :]
        kt = k_ref[0, pl.ds(k0, BAND_WIN), :]
        vt = v_ref[0, pl.ds(k0, BAND_WIN), :]
        accs = []
        for a in range(2):
            s = _dot_nt(qp * hm_ref[0, a:a + 1, :], kt) + bias_ref[a, case]
            m = jnp.max(s, axis=-1, keepdims=True)
            pe = jnp.exp2(s - m).astype(BF16)
            accs.append(_dot(pe, _ones_outside(own[a], vt)))
        o = _normalize_pair(accs[0], accs[1], lane)
        o_ref[0, pl.ds(q0, BAND_TQ), :] = o.astype(o_ref.dtype)
        return 0

    lax.fori_loop(0, S // BAND_TQ, q_block, 0)


def _band_attn(q, k, v, hm, bias):
    B, S, _ = v.shape
    spec = pl.BlockSpec((1, S, PAIR_W), lambda p, b: (b, 0, p))
    return pl.pallas_call(
        functools.partial(_band_kernel, S),
        out_shape=jax.ShapeDtypeStruct((B, S, MIX_W), BF16),
        grid=(2, B),
        in_specs=[spec, spec, spec, pl.BlockSpec((1, 2, PAIR_W), lambda p, b: (p, 0, 0)),
                  pl.BlockSpec((2,) + bias.shape[1:], lambda p, b: (p, 0, 0, 0))],
        out_specs=spec,
        name="attn_band",
        compiler_params=_cparams(("arbitrary", "arbitrary")),
    )(q, k, v, hm, bias)


def _stick_kernel(S, q_ref, k_ref, v_ref, hm_ref, uo_ref, o_ref):
    T = ATT_BLOCK
    row = lax.broadcasted_iota(jnp.int32, (T, T), 0)
    col = lax.broadcasted_iota(jnp.int32, (T, T), 1)
    strict = col < row
    lane = lax.broadcasted_iota(jnp.int32, (T, PAIR_W), 1)

    def q_block(qi, _):
        q0 = pl.multiple_of(qi * T, T)
        qp = q_ref[0, pl.ds(q0, T), :]
        qm = [qp * hm_ref[0, a:a + 1, :] for a in range(2)]

        def step(kj, carry, diag):
            k0 = pl.multiple_of(kj * T, T)
            kt = k_ref[0, pl.ds(k0, T), :]
            vt = v_ref[0, pl.ds(k0, T), :]
            nrb = T // ROW_BLOCK
            chains = [(a, rb) for a in range(2) for rb in range(nrb)]
            rs = [slice(rb * ROW_BLOCK, (rb + 1) * ROW_BLOCK) for rb in range(nrb)]
            kw = [(rb + 1) * ROW_BLOCK if diag else T for rb in range(nrb)]
            z = [_dot_nt(qm[a][rs[rb]], kt[:kw[rb]]) for a, rb in chains]
            sp = []
            for (a, rb), zc in zip(chains, z):
                s_ = jnp.maximum(zc, 0.0) + jnp.log2(1.0 + jnp.exp2(-jnp.abs(zc)))
                if diag:
                    s_ = jnp.where(strict[rs[rb], :kw[rb]], s_, 0.0)
                sp.append(s_)
            cum = []
            for (a, rb), s_ in zip(chains, sp):
                sb = s_.astype(BF16)
                cum.append([_dot(sb[:, j * LANES:(j + 1) * LANES], uo_ref[...]) for j in range(kw[rb] // LANES)])
            w, runs = [], []
            for (a, rb), zc, s_, cm in zip(chains, z, sp, cum):
                right = [None] * len(cm)
                run = carry[a][1][rs[rb]]
                for j in reversed(range(len(cm))):
                    right[j] = run
                    run = run + cm[j][:, LANES:]
                w_ = jnp.exp2(zc - s_ - jnp.concatenate([c[:, :LANES] for c in cm], axis=1)
                              - jnp.concatenate(right, axis=1))
                if diag:
                    w_ = jnp.where(strict[rs[rb], :kw[rb]], w_, 0.0)
                w.append(w_.astype(BF16))
                runs.append(run)
            accs = [carry[a][0][rs[rb]] + _dot(w_, vt[:kw[rb]]) for (a, rb), w_ in zip(chains, w)]
            return tuple((jnp.concatenate(accs[a * nrb:(a + 1) * nrb], axis=0),
                          jnp.concatenate(runs[a * nrb:(a + 1) * nrb], axis=0)) for a in range(2))

        init = tuple((jnp.zeros((T, PAIR_W), F32), jnp.zeros((T, LANES), F32)) for _ in range(2))
        carry = step(qi, init, True)
        carry = lax.fori_loop(0, qi, lambda i, c: step(qi - 1 - i, c, False), carry)
        (a0, _), (a1, _) = carry
        o = jnp.where(lane < HEAD_DIM, a0, a1)
        o_ref[0, pl.ds(q0, T), :] = o.astype(o_ref.dtype)
        return 0

    lax.fori_loop(0, S // T, q_block, 0)


def _stick_attn(q, k, v, hm, uo):
    B, S, _ = v.shape
    spec = pl.BlockSpec((1, S, PAIR_W), lambda b, p: (b, 0, p))
    return pl.pallas_call(
        functools.partial(_stick_kernel, S),
        out_shape=jax.ShapeDtypeStruct((B, S, MIX_W), BF16),
        grid=(B, 2),
        in_specs=[spec, spec, spec, pl.BlockSpec((1, 2, PAIR_W), lambda b, p: (p, 0, 0)),
                  pl.BlockSpec(uo.shape, lambda b, p: (0, 0))],
        out_specs=spec,
        name="attn_stick",
        compiler_params=_cparams(("arbitrary", "arbitrary")),
    )(q, k, v, hm, uo)


def _route(scores_t, sel_t):
    G, E = N_GROUPS, EXPERTS_PER_GROUP
    sel = [sel_t[e:e + 1, :] for e in range(G * E)]
    sco = [scores_t[e:e + 1, :] for e in range(G * E)]
    gs = []
    for g in range(G):
        a, b, c, d = sel[E * g:E * g + E]
        gs.append(jnp.maximum(jnp.maximum(jnp.maximum(a + b, a + c), jnp.maximum(a + d, b + c)),
                              jnp.maximum(b + d, c + d)))
    gmax = jnp.maximum(jnp.maximum(gs[0], gs[1]), jnp.maximum(gs[2], gs[3]))
    grp = jnp.where(gs[0] == gmax, 0, jnp.where(gs[1] == gmax, 1, jnp.where(gs[2] == gmax, 2, 3)))

    def pick(rows, j):
        return jnp.where(grp == 0, rows[j], jnp.where(grp == 1, rows[E + j],
                                                      jnp.where(grp == 2, rows[2 * E + j], rows[3 * E + j])))

    ig = [pick(sel, j) for j in range(E)]
    igs = [pick(sco, j) for j in range(E)]

    def first_argmax(v):
        mx = jnp.maximum(jnp.maximum(v[0], v[1]), jnp.maximum(v[2], v[3]))
        return jnp.where(v[0] == mx, 0, jnp.where(v[1] == mx, 1, jnp.where(v[2] == mx, 2, 3)))

    l1 = first_argmax(ig)
    ig2 = [jnp.where(l1 == j, -jnp.inf, ig[j]) for j in range(E)]
    l2 = first_argmax(ig2)

    def take(v, idx):
        return jnp.where(idx == 0, v[0], jnp.where(idx == 1, v[1], jnp.where(idx == 2, v[2], v[3])))

    w1 = take(igs, l1)
    w2 = take(igs, l2)
    den = w1 + w2
    w1 = w1 / den
    w2 = w2 / den
    out = []
    for g in range(G):
        for j in range(E):
            cj = jnp.where(l1 == j, w1, jnp.where(l2 == j, w2, 0.0))
            out.append(jnp.where(grp == g, cj, 0.0))
    return out


def _k5_kernel(x_ref, mod_ref, ng1_ref, ng2_ref, oa_ref, ob_ref, oc_ref, od_ref,
               wg_ref, wb_ref, wo_ref, wr_ref, rb_ref, x1_ref, h2_ref, cmb_ref):
    x = x_ref[0]
    D = x.shape[1]
    tm = x.shape[0]
    sh, sc, g_m = mod_ref[0, 0:1, :], mod_ref[0, 1:2, :], mod_ref[0, 2:3, :]
    sh_f, sc_f = mod_ref[0, 3:4, :], mod_ref[0, 4:5, :]
    hb = (_rms(x, ng1_ref[...]) * (1.0 + sc) + sh).astype(BF16)
    merged = None
    for b, o_ref in enumerate((oa_ref, ob_ref, oc_ref, od_ref)):
        gate = _sigmoid(_dot(hb, wg_ref[:, b * D:(b + 1) * D]))
        t = gate * _dot(o_ref[0], wb_ref[b])
        merged = t if merged is None else merged + t
    x1 = x + g_m * _dot(merged.astype(BF16), wo_ref[...])
    x1_ref[0] = x1
    h2 = (_rms(x1, ng2_ref[...]) * (1.0 + sc_f) + sh_f).astype(BF16)
    h2_ref[0] = h2
    scores = _sigmoid(_dot(h2, wr_ref[...]))
    st = scores.T[0:N_EXPERTS, :]
    rows = _route(st, st + rb_ref[...])
    nr = N_EXPERTS + 8
    rid = lax.broadcasted_iota(jnp.int32, (nr, tm), 0)
    top = jnp.where(rid == N_EXPERTS, 1.0, 0.0)
    for e in range(N_EXPERTS):
        top = jnp.where(rid == e, rows[e], top)
    cmb_ref[0] = jnp.concatenate([top, jnp.zeros((LANES - nr, tm), F32)], axis=0).T


def _k5(x, mod, ng1, ng2, oa, ob, oc, od, lw, wr, rb, tm):
    B, S, D = x.shape
    bs = lambda w: pl.BlockSpec((1, tm, w), lambda b, i: (b, i, 0))
    full = lambda a: pl.BlockSpec(a.shape, lambda b, i: (0,) * a.ndim)
    consts = (lw["wg"], lw["wb"], lw["wo"], wr, rb)
    return pl.pallas_call(
        _k5_kernel,
        out_shape=(jax.ShapeDtypeStruct((B, S, D), F32), jax.ShapeDtypeStruct((B, S, D), BF16),
                   jax.ShapeDtypeStruct((B, S, LANES), F32)),
        grid=(B, S // tm),
        in_specs=[bs(D), pl.BlockSpec((1, 6, D), lambda b, i: (b, 0, 0)), full(ng1), full(ng2),
                  bs(MIX_W), bs(MIX_W), bs(MIX_W), bs(MIX_W)] + [full(a) for a in consts],
        out_specs=(bs(D), bs(D), bs(LANES)),
        name="merge_route",
        compiler_params=_cparams(("arbitrary", "arbitrary")),
    )(x, mod, ng1, ng2, oa, ob, oc, od, *consts)


def _swiglu_rows(h2_ref, wgu, wd, acc_ref, weight_fn, first):
    tm = h2_ref.shape[0]
    chunks = [slice(r, r + MOE_ROW_BLOCK) for r in range(0, tm, MOE_ROW_BLOCK)]

    def up(rs):
        h = h2_ref[rs, :]
        return [_dot(h, w) for w in wgu]

    nxt = up(chunks[0])
    for i, rs in enumerate(chunks):
        gu = nxt
        if i + 1 < len(chunks):
            nxt = up(chunks[i + 1])
        acts = []
        for j, gu_j in enumerate(gu):
            g = gu_j[:, :D_EXPERT]
            a = g * _sigmoid(g) * gu_j[:, D_EXPERT:]
            w = weight_fn(rs, j)
            acts.append((a if w is None else a * w).astype(BF16))
        y = _dot(acts[0] if len(acts) == 1 else jnp.concatenate(acts, axis=1), wd)
        if first:
            acc_ref[rs, :] = y
        else:
            acc_ref[rs, :] += y


def _moe_kernel(h2_ref, cmb_ref, x1_ref, mod_ref, wgu_ref, wd_ref, wgu_s_ref, wd_s_ref, out_ref, acc_ref):
    g = pl.program_id(1)
    E = EXPERTS_PER_GROUP
    lane = lax.broadcasted_iota(jnp.int32, (MOE_ROW_BLOCK, LANES), 1)

    @pl.when(g == 0)
    def _():
        _swiglu_rows(h2_ref, [wgu_s_ref[...]], wd_s_ref[...], acc_ref, lambda rs, j: None, True)

    def combine(rs, j):
        return jnp.sum(jnp.where(lane == g * E + j, cmb_ref[rs, :], 0.0), axis=-1, keepdims=True)

    _swiglu_rows(h2_ref, [wgu_ref[0, j] for j in range(E)], wd_ref[0], acc_ref, combine, False)

    @pl.when(g == pl.num_programs(1) - 1)
    def _():
        out_ref[...] = x1_ref[...] + mod_ref[0, 5:6, :] * acc_ref[...]


def _moe(h2, cmb, x1, mod, wgu, wd, wgu_s, wd_s, S, tm):
    N, D = h2.shape
    G, E = wgu.shape[0], wgu.shape[1]
    per_b = S // tm
    row = lambda w: pl.BlockSpec((tm, w), lambda i, g: (i, 0))
    return pl.pallas_call(
        _moe_kernel,
        out_shape=jax.ShapeDtypeStruct((N, D), F32),
        grid=(N // tm, G),
        in_specs=[row(D), row(LANES), row(D),
                  pl.BlockSpec((1, 6, D), lambda i, g: (i // per_b, 0, 0)),
                  pl.BlockSpec((1, E, D, 2 * D_EXPERT), lambda i, g: (g, 0, 0, 0)),
                  pl.BlockSpec((1, E * D_EXPERT, D), lambda i, g: (g, 0, 0)),
                  pl.BlockSpec((D, 2 * D_EXPERT), lambda i, g: (0, 0)),
                  pl.BlockSpec((D_EXPERT, D), lambda i, g: (0, 0))],
        out_specs=row(D),
        scratch_shapes=[pltpu.VMEM((tm, D), F32)],
        name="moe",
        compiler_params=_cparams(("arbitrary", "arbitrary")),
    )(h2, cmb, x1, mod, wgu, wd, wgu_s, wd_s)


def _const_tables():
    lane = np.arange(MIX_W)
    bd = (lane[:, None] // HEAD_DIM == lane[None, :] // HEAD_DIM).astype(np.float32)
    mp = np.zeros((2, MIX_W, MIX_W), np.float32)
    hm_c = np.zeros((2, 2, MIX_W), np.float32)
    for p in range(2):
        head = np.where(lane < LANES, 2 * p + lane // HEAD_DIM, ((lane - LANES) % HEAD_DIM) // C_HALF)
        mp[p] = head[:, None] == head[None, :]
        for a in range(2):
            hm_c[p, a] = head == 2 * p + a
    hm = np.zeros((2, 2, PAIR_W), np.float32)
    for a in range(2):
        hm[:, a] = (np.arange(PAIR_W) // HEAD_DIM == a)
    j = np.arange(LANES)
    ustrict = (j[:, None] > j[None, :]).astype(np.float32)
    uo = np.concatenate([ustrict, np.ones((LANES, LANES), np.float32)], axis=1)
    return (jnp.asarray(bd, BF16), jnp.asarray(mp, BF16), jnp.asarray(hm, BF16),
            jnp.asarray(hm_c, BF16), jnp.asarray(uo, BF16))


def _band_bias(rel_bias):
    L, H, _ = rel_bias.shape
    R, Wn = BAND_TQ, BAND_WIN
    ncase = BAND_LEFT // BAND_TQ + 1
    off = (np.arange(ncase) * BAND_TQ)[:, None]
    n = R + Wn
    k = np.arange(n)[None, :]
    dist = np.where(k <= Wn, off - k, off + n - k)
    idx = np.clip(dist, -D_MAX_REL, D_MAX_REL) + D_MAX_REL
    u = rel_bias[:, :, jnp.asarray(idx)]
    bias = jnp.tile(u, (1, 1, 1, R))[..., :R * (n - 1)].reshape(L, H, ncase, R, n - 1)[..., :Wn]
    r = np.arange(R)[None, :, None]
    c = np.arange(Wn)[None, None, :]
    kc = np.floor_divide(c - off[:, :, None], CHUNK)
    rc = r // CHUNK
    vis = (kc <= rc) & (kc >= rc - D_LEFT_CHUNKS)
    return jnp.where(jnp.asarray(vis)[None, None], bias * LOG2E, NEG_INF).astype(F32)


def _pack_weights(w_in, fox_forget_b, fox_q_g, fox_k_g, mla_cq_g, mla_ckv_g, mla_w_uq, mla_w_ukv,
                  mla_q_g, mla_k_g, chunk_q_g, chunk_k_g, w_branch, w_out):
    L, D, _ = w_in.shape
    W = MIX_W
    a0, b0 = 0, 3 * W + N_HEADS
    c0 = b0 + 3 * W
    d0 = c0 + C_Q_RANK + C_KV_RANK + C_ROPE
    g0 = d0 + 3 * W
    scale = HEAD_DIM ** -0.5
    kr = w_in[:, :, c0 + C_Q_RANK + C_KV_RANK:d0]
    w1 = jnp.concatenate([
        w_in[:, :, a0:a0 + 3 * W],
        w_in[:, :, b0:b0 + W] * (scale * LOG2E), w_in[:, :, b0 + W:b0 + 3 * W],
        w_in[:, :, d0:d0 + 3 * W],
        w_in[:, :, c0:c0 + C_Q_RANK + C_KV_RANK],
        jnp.tile(kr[:, :, :C_HALF], (1, 1, N_HEADS)), jnp.tile(kr[:, :, C_HALF:], (1, 1, N_HEADS)),
    ], axis=2).astype(BF16)
    wfa = jnp.pad(jnp.swapaxes(w_in[:, :, 3 * W:3 * W + N_HEADS], 1, 2), ((0, 0), (0, 8 - N_HEADS), (0, 0))).astype(BF16)
    fb = jnp.pad(fox_forget_b, ((0, 0), (0, 8 - N_HEADS))).reshape(L, 8, 1).astype(F32)
    wg = w_in[:, :, g0:].astype(BF16)

    uq = mla_w_uq.reshape(L, C_Q_RANK, N_HEADS, C_QK)
    uq_rope = jnp.concatenate([uq[..., C_NOPE:C_NOPE + C_HALF].reshape(L, C_Q_RANK, N_HEADS * C_HALF),
                               uq[..., C_NOPE + C_HALF:].reshape(L, C_Q_RANK, N_HEADS * C_HALF)], axis=2)
    wuq = jnp.concatenate([uq[:, :, 0, :C_NOPE], uq[:, :, 1, :C_NOPE], uq_rope,
                           uq[:, :, 2, :C_NOPE], uq[:, :, 3, :C_NOPE], uq_rope], axis=2).astype(BF16)
    ukv = mla_w_ukv.reshape(L, C_KV_RANK, N_HEADS, C_NOPE + HEAD_DIM)
    wukv = jnp.concatenate([ukv[..., :C_NOPE].reshape(L, C_KV_RANK, W),
                            ukv[..., C_NOPE:].reshape(L, C_KV_RANK, W)], axis=2).astype(BF16)

    def c_gain(g, s):
        nope = g[:, :C_NOPE]
        rope = jnp.concatenate([jnp.tile(g[:, C_NOPE:C_NOPE + C_HALF], (1, N_HEADS)),
                                jnp.tile(g[:, C_NOPE + C_HALF:], (1, N_HEADS))], axis=1)
        pair = jnp.concatenate([nope, nope, rope], axis=1)
        return jnp.concatenate([pair, pair], axis=1) * s

    def pad512(v):
        return jnp.pad(v, ((0, 0), (0, 2 * W - v.shape[1])))

    gv = jnp.stack([
        pad512(jnp.tile(fox_q_g, (1, N_HEADS)) * (scale * LOG2E)), pad512(jnp.tile(fox_k_g, (1, N_HEADS))),
        pad512(jnp.tile(chunk_q_g, (1, N_HEADS)) * (scale * LOG2E)), pad512(jnp.tile(chunk_k_g, (1, N_HEADS))),
        pad512(mla_cq_g), pad512(mla_ckv_g), c_gain(mla_q_g, C_QK ** -0.5 * LOG2E), c_gain(mla_k_g, 1.0),
    ], axis=1).astype(F32)
    return dict(w1=w1, wfa=wfa, fb=fb, wg=wg, wuq=wuq, wukv=wukv, gv=gv,
                wb=w_branch.astype(BF16), wo=w_out.astype(BF16))


def kernel(x, c, positions, norm_mix_g, norm_ffn_g, w_ada, b_ada, w_in, fox_forget_b, fox_q_g, fox_k_g, mla_cq_g, mla_ckv_g, mla_w_uq, mla_w_ukv, mla_q_g, mla_k_g, chunk_q_g, chunk_k_g, chunk_rel_bias, w_branch, w_out, router_w, router_b, exp_w_gate, exp_w_up, exp_w_down, sh_w_gate, sh_w_up, sh_w_down):
    B, S, D = x.shape
    L = w_in.shape[0]
    assert S % ATT_BLOCK == 0 and S >= BAND_WIN and D % LANES == 0
    tm_p1 = min(S, 512)
    tm_k5 = min(S, 256)
    tm_moe = min(S, 1024)

    bd, mp, hm, hm_c, uo = _const_tables()
    pw = _pack_weights(w_in, fox_forget_b, fox_q_g, fox_k_g, mla_cq_g, mla_ckv_g, mla_w_uq, mla_w_ukv,
                       mla_q_g, mla_k_g, chunk_q_g, chunk_k_g, w_branch, w_out)
    band_bias = _band_bias(chunk_rel_bias)
    wr = jnp.pad(router_w, ((0, 0), (0, LANES - N_EXPERTS))).astype(BF16)
    rb = router_b.reshape(N_EXPERTS, 1).astype(F32)
    wgu = jnp.concatenate([exp_w_gate, exp_w_up], axis=-1).astype(BF16).reshape(
        L, N_GROUPS, EXPERTS_PER_GROUP, D, 2 * D_EXPERT)
    wd = exp_w_down.astype(BF16).reshape(L, N_GROUPS, EXPERTS_PER_GROUP * D_EXPERT, D)
    wgu_s = jnp.concatenate([sh_w_gate, sh_w_up], axis=-1).astype(BF16)
    wd_s = sh_w_down.astype(BF16)

    mod = _modulation(c, w_ada, b_ada).reshape(L, B, 6, D)
    cosl, sinl = _rope_tables(positions)

    for l in range(L):
        lw = {k: v[l] for k, v in pw.items()}
        lw["bd"], lw["mp"] = bd, mp
        ng1 = norm_mix_g[l].reshape(1, D)
        ng2 = norm_ffn_g[l].reshape(1, D)
        (qa, ka, va, lf, qb, kb, vb, qc, kc, vc, qd, kd, vd) = _p1(x, mod[l], ng1, lw, cosl, sinl, tm_p1)
        oa = _softmax_attn("fox", qa, ka, va, hm, lf)
        ob = _stick_attn(qb, kb, vb, hm, uo)
        oc = _softmax_attn("mla", qc, kc, vc, hm_c)
        od = _band_attn(qd, kd, vd, hm, band_bias[l])
        x1, h2, cmb = _k5(x, mod[l], ng1, ng2, oa, ob, oc, od, lw, wr, rb, tm_k5)
        x = _moe(h2.reshape(B * S, D), cmb.reshape(B * S, LANES), x1.reshape(B * S, D), mod[l],
                 wgu[l], wd[l], wgu_s[l], wd_s[l], S, tm_moe).reshape(B, S, D)
    return x
```

```python
import functools
import math

import numpy as np
import jax
import jax.numpy as jnp
from jax import lax
from jax.experimental import pallas as pl
from jax.experimental.pallas import tpu as pltpu

F32 = jnp.float32
BF16 = jnp.bfloat16

HEAD_DIM = 64
N_HEADS = 4
MIX_W = N_HEADS * HEAD_DIM
CHUNK = 64
NEG_INF = -1e30
EPS = 1e-6
C_Q_RANK = 256
C_KV_RANK = 128
C_NOPE = 64
C_ROPE = 32
C_HALF = C_ROPE // 2
C_QK = C_NOPE + C_ROPE
ROPE_BASE = 10000.0
D_LEFT_CHUNKS = 8
D_MAX_REL = 128
N_BRANCH = 4
N_EXPERTS = 16
N_GROUPS = 4
EXPERTS_PER_GROUP = 4
D_EXPERT = 256
LOG2E = math.log2(math.e)

LANES = 128
ATT_BLOCK = 512
ROW_BLOCK = 256
SOFTMAX_ROW_BLOCK = 512
MOE_ROW_BLOCK = 256
MOE_TOKEN_TILE = 256
MOE_ALIGN = 16
MOE_CHUNK = 32
MOE_LOCAL_ROWS = MOE_TOKEN_TILE + N_GROUPS * MOE_CHUNK
MOE_EXPERT_TILE = 512
CW_GROUP_LANE = 3 * EXPERTS_PER_GROUP
BAND_TQ = 256
BAND_LEFT = D_LEFT_CHUNKS * CHUNK
BAND_WIN = BAND_LEFT + BAND_TQ
PAIR_W = 2 * HEAD_DIM
VMEM_LIMIT = 56 * 1024 * 1024

P1_COLS = 9 * MIX_W + C_Q_RANK + C_KV_RANK + LANES


def _cparams(sem):
    return pltpu.CompilerParams(dimension_semantics=sem, vmem_limit_bytes=VMEM_LIMIT)


def _dot(a, b):
    return jnp.dot(a, b, preferred_element_type=F32)


def _dot_nt(a, b):
    return lax.dot_general(a, b, (((1,), (1,)), ((), ())), preferred_element_type=F32)


def _sigmoid(x):
    return 1.0 / (1.0 + jnp.exp(-x))


def _rms(x, g):
    return x * lax.rsqrt(jnp.mean(x * x, axis=-1, keepdims=True) + EPS) * g


def _mod_kernel(c_ref, w_ref, b_ref, o_ref):
    c = c_ref[...]
    cond = c * _sigmoid(c)
    o_ref[0] = jnp.dot(cond, w_ref[0], preferred_element_type=F32,
                       precision=lax.Precision.HIGHEST) + b_ref[0]


def _modulation(c, w_ada, b_ada):
    L, D, D6 = w_ada.shape
    B = c.shape[0]
    tn = 1024
    return pl.pallas_call(
        _mod_kernel,
        out_shape=jax.ShapeDtypeStruct((L, B, D6), F32),
        grid=(L, D6 // tn),
        in_specs=[pl.BlockSpec((B, D), lambda l, j: (0, 0)),
                  pl.BlockSpec((1, D, tn), lambda l, j: (l, 0, j)),
                  pl.BlockSpec((1, 1, tn), lambda l, j: (l, 0, j))],
        out_specs=pl.BlockSpec((1, B, tn), lambda l, j: (l, 0, j)),
        name="adaln_mod",
        compiler_params=_cparams(("arbitrary", "arbitrary")),
    )(c, w_ada, b_ada.reshape(L, 1, D6))


def _rope_kernel(ang_ref, sign_ref, cos_ref, sin_ref):
    a = ang_ref[0]
    cos_ref[0] = jnp.cos(a)
    sin_ref[0] = jnp.sin(a) * sign_ref[...]


def _rope_tables(positions):
    B, S = positions.shape
    inv_freq = jnp.power(ROPE_BASE, -jnp.arange(C_HALF, dtype=F32) / C_HALF)
    inv_l = jnp.tile(inv_freq, LANES // C_HALF)
    ang = positions.astype(F32)[:, :, None] * inv_l[None, None, :]
    sign = jnp.where(jnp.arange(LANES) < LANES // 2, -1.0, 1.0).astype(F32).reshape(1, LANES)
    ts = min(S, 1024)
    spec = pl.BlockSpec((1, ts, LANES), lambda b, i: (b, i, 0))
    return pl.pallas_call(
        _rope_kernel,
        out_shape=(jax.ShapeDtypeStruct((B, S, LANES), F32),) * 2,
        grid=(B, S // ts),
        in_specs=[spec, pl.BlockSpec((1, LANES), lambda b, i: (0, 0))],
        out_specs=(spec, spec),
        name="rope_tables",
        compiler_params=_cparams(("arbitrary", "arbitrary")),
    )(ang, sign)


def _rope(t, cos, sin):
    return t * cos + pltpu.roll(t, LANES // 2, 1) * sin


def _p1_kernel(x_ref, mod_ref, ng_ref, w1_ref, wfa_ref, fb_ref, gv_ref, wuq_ref, wukv_ref,
               bd_ref, mp_ref, cos_ref, sin_ref,
               qa_ref, ka_ref, va_ref, lf_ref, qb_ref, kb_ref, vb_ref,
               qc_ref, kc_ref, vc_ref, qd_ref, kd_ref, vd_ref):
    x = x_ref[0]
    sh = mod_ref[0, 0:1, :]
    sc = mod_ref[0, 1:2, :]
    hb = (_rms(x, ng_ref[...]) * (1.0 + sc) + sh).astype(BF16)
    W = MIX_W

    def mm(c0, c1):
        return _dot(hb, w1_ref[:, c0:c1])

    def head_norm(t, g):
        ss = _dot((t * t).astype(BF16), bd_ref[...])
        return (t * lax.rsqrt(ss * (1.0 / HEAD_DIM) + EPS) * g).astype(BF16)

    r = mm(0, 3 * W)
    qa_ref[0] = head_norm(r[:, 0:W], gv_ref[0:1, 0:W])
    ka_ref[0] = head_norm(r[:, W:2 * W], gv_ref[1:2, 0:W])
    va_ref[0] = r[:, 2 * W:3 * W].astype(BF16)
    fa = _dot_nt(wfa_ref[...], hb) + fb_ref[...]
    lf_ref[0] = jnp.minimum(fa, 0.0) - jnp.log(1.0 + jnp.exp(-jnp.abs(fa)))

    r = mm(3 * W, 6 * W)
    qb_ref[0] = r[:, 0:W].astype(BF16)
    kb_ref[0] = r[:, W:2 * W].astype(BF16)
    vb_ref[0] = r[:, 2 * W:3 * W].astype(BF16)

    r = mm(6 * W, 9 * W)
    qd_ref[0] = head_norm(r[:, 0:W], gv_ref[2:3, 0:W])
    kd_ref[0] = head_norm(r[:, W:2 * W], gv_ref[3:4, 0:W])
    vd_ref[0] = r[:, 2 * W:3 * W].astype(BF16)

    c0 = 9 * W
    r = mm(c0, c0 + C_Q_RANK + C_KV_RANK + LANES)
    cos = cos_ref[0]
    sin = sin_ref[0]
    cq = _rms(r[:, 0:C_Q_RANK], gv_ref[4:5, 0:C_Q_RANK]).astype(BF16)
    ckv = _rms(r[:, C_Q_RANK:C_Q_RANK + C_KV_RANK], gv_ref[5:6, 0:C_KV_RANK]).astype(BF16)
    kr = _rope(r[:, C_Q_RANK + C_KV_RANK:], cos, sin)
    qu = _dot(cq, wuq_ref[...])
    kvu = _dot(ckv, wukv_ref[...])
    vc_ref[0] = kvu[:, 2 * LANES:].astype(BF16)
    for p in range(2):
        qp = jnp.concatenate([qu[:, p * 2 * LANES:p * 2 * LANES + LANES],
                              _rope(qu[:, p * 2 * LANES + LANES:(p + 1) * 2 * LANES], cos, sin)], axis=1)
        kp = jnp.concatenate([kvu[:, p * LANES:(p + 1) * LANES], kr], axis=1)
        for t, gr, dst in ((qp, 6, qc_ref), (kp, 7, kc_ref)):
            ss = _dot((t * t).astype(BF16), mp_ref[p])
            tn = t * lax.rsqrt(ss * (1.0 / C_QK) + EPS) * gv_ref[gr:gr + 1, p * 2 * LANES:(p + 1) * 2 * LANES]
            dst[0, :, p * 2 * LANES:(p + 1) * 2 * LANES] = tn.astype(BF16)


def _p1(x, mod, ng, lw, cosl, sinl, tm):
    B, S, D = x.shape
    W = MIX_W
    bs = lambda w: pl.BlockSpec((1, tm, w), lambda b, i: (b, i, 0))
    full = lambda a: pl.BlockSpec(a.shape, lambda b, i: (0,) * a.ndim)
    o256 = jax.ShapeDtypeStruct((B, S, W), BF16)
    o512 = jax.ShapeDtypeStruct((B, S, 2 * W), BF16)
    out_shape = (o256, o256, o256, jax.ShapeDtypeStruct((B, 8, S), F32),
                 o256, o256, o256, o512, o512, o256, o256, o256, o256)
    out_specs = (bs(W), bs(W), bs(W), pl.BlockSpec((1, 8, tm), lambda b, i: (b, 0, i)),
                 bs(W), bs(W), bs(W), bs(2 * W), bs(2 * W), bs(W), bs(W), bs(W), bs(W))
    consts = (ng, lw["w1"], lw["wfa"], lw["fb"], lw["gv"], lw["wuq"], lw["wukv"], lw["bd"], lw["mp"])
    return pl.pallas_call(
        _p1_kernel,
        out_shape=out_shape,
        grid=(B, S // tm),
        in_specs=[bs(D), pl.BlockSpec((1, 6, D), lambda b, i: (b, 0, 0))]
                 + [full(a) for a in consts] + [bs(LANES), bs(LANES)],
        out_specs=out_specs,
        name="in_proj",
        compiler_params=_cparams(("arbitrary", "arbitrary")),
    )(x, mod, *consts, cosl, sinl)


def _lane_cumsum(x):
    n = x.shape[1]
    idx = lax.broadcasted_iota(jnp.int32, x.shape, 1)
    k = 1
    while k < n:
        x = x + jnp.where(idx >= k, pltpu.roll(x, k, 1), 0.0)
        k *= 2
    return x


def _ones_outside(own, v):
    return jnp.where(own[:v.shape[0]], v, jnp.ones_like(v))


def _normalize_pair(acc0, acc1, lane):
    half = PAIR_W // 2
    o0 = acc0 / pltpu.roll(acc0, half, 1)
    o1 = acc1 / pltpu.roll(acc1, half, 1)
    return jnp.where(lane < HEAD_DIM, o0, o1)


def _softmax_attn_kernel(mode, S, *refs):
    T = ATT_BLOCK
    if mode == "fox":
        q_ref, k_ref, v_ref, hm_ref, lf_ref, o_ref, cf_ref = refs
        cf = _lane_cumsum(lf_ref[0]) * LOG2E
        for h in range(N_HEADS):
            cf_ref[h] = cf[h:h + 1, :]
    else:
        q_ref, k_ref, v_ref, hm_ref, o_ref = refs
    pair = pl.program_id(1)
    row = lax.broadcasted_iota(jnp.int32, (T, T), 0)
    col = lax.broadcasted_iota(jnp.int32, (T, T), 1)
    if mode == "fox":
        dmask = col <= row
    else:
        dmask = (col // CHUNK) <= (row // CHUNK)
    lane = lax.broadcasted_iota(jnp.int32, (T, PAIR_W), 1)
    own = [lane // HEAD_DIM == a for a in range(2)]

    def q_block(qi, _):
        q0 = pl.multiple_of(qi * T, T)
        qp = q_ref[0, pl.ds(q0, T), :]
        qm = [qp * hm_ref[0, a:a + 1, :] for a in range(2)]

        def step(kj, carry, diag):
            k0 = pl.multiple_of(kj * T, T)
            kt = k_ref[0, pl.ds(k0, T), :]
            vt = v_ref[0, pl.ds(k0, T), :]
            nrb = T // SOFTMAX_ROW_BLOCK
            chains = [(a, rb) for a in range(2) for rb in range(nrb)]
            rs = [slice(rb * SOFTMAX_ROW_BLOCK, (rb + 1) * SOFTMAX_ROW_BLOCK) for rb in range(nrb)]
            kw = [(rb + 1) * SOFTMAX_ROW_BLOCK if diag else T for rb in range(nrb)]
            s = [_dot_nt(qm[a][rs[rb]], kt[:kw[rb]]) for a, rb in chains]
            if mode == "fox":
                cfk = [cf_ref[2 * pair + a, :, pl.ds(k0, T)] for a in range(2)]
                s = [sc - cfk[a][:, :kw[rb]] for (a, rb), sc in zip(chains, s)]
            if diag:
                s = [jnp.where(dmask[rs[rb], :kw[rb]], sc, NEG_INF) for (a, rb), sc in zip(chains, s)]
            m_new = [jnp.maximum(c[0], jnp.max(sc, axis=-1, keepdims=True)) for c, sc in zip(carry, s)]
            pe = [jnp.exp2(sc - mn).astype(BF16) for sc, mn in zip(s, m_new)]
            va = [_ones_outside(own[a], vt) for a in range(2)]
            acc = [jnp.exp2(c[0] - mn) * c[1] + _dot(p_, va[a][:kw[rb]])
                   for (a, rb), c, mn, p_ in zip(chains, carry, m_new, pe)]
            return tuple(zip(m_new, acc))

        nch = 2 * (T // SOFTMAX_ROW_BLOCK)
        init = tuple((jnp.full((SOFTMAX_ROW_BLOCK, 1), NEG_INF, F32),
                      jnp.zeros((SOFTMAX_ROW_BLOCK, PAIR_W), F32)) for _ in range(nch))
        carry = lax.fori_loop(0, qi, lambda kj, c: step(kj, c, False), init)
        carry = step(qi, carry, True)
        accs = [jnp.concatenate([c[1] for c in carry[a * nch // 2:(a + 1) * nch // 2]], axis=0) for a in range(2)]
        o = _normalize_pair(accs[0], accs[1], lane)
        o_ref[0, pl.ds(q0, T), :] = o.astype(o_ref.dtype)
        return 0

    lax.fori_loop(0, S // T, q_block, 0)


def _softmax_attn(mode, q, k, v, hm, lf=None):
    B, S, _ = v.shape
    cw = q.shape[2] // 2
    qspec = pl.BlockSpec((1, S, cw), lambda b, p: (b, 0, p))
    vspec = pl.BlockSpec((1, S, PAIR_W), lambda b, p: (b, 0, p))
    in_specs = [qspec, qspec, vspec, pl.BlockSpec((1, 2, cw), lambda b, p: (p, 0, 0))]
    args = [q, k, v, hm]
    scratch = []
    if mode == "fox":
        in_specs.append(pl.BlockSpec((1, 8, S), lambda b, p: (b, 0, 0)))
        args.append(lf)
        scratch = [pltpu.VMEM((N_HEADS, 1, S), F32)]
    return pl.pallas_call(
        functools.partial(_softmax_attn_kernel, mode, S),
        out_shape=jax.ShapeDtypeStruct((B, S, MIX_W), BF16),
        grid=(B, 2),
        in_specs=in_specs,
        out_specs=vspec,
        scratch_shapes=scratch,
        name="attn_" + mode,
        compiler_params=_cparams(("arbitrary", "arbitrary")),
    )(*args)


def _band_kernel(S, q_ref, k_ref, v_ref, hm_ref, bias_ref, o_ref):
    lane = lax.broadcasted_iota(jnp.int32, (BAND_TQ, PAIR_W), 1)
    vlane = lax.broadcasted_iota(jnp.int32, (BAND_WIN, PAIR_W), 1)
    own = [vlane // HEAD_DIM == a for a in range(2)]

    def q_block(qi, _):
        q0 = pl.multiple_of(qi * BAND_TQ, BAND_TQ)
        k0 = pl.multiple_of(jnp.maximum(q0 - BAND_LEFT, 0), BAND_TQ)
        case = jnp.minimum(qi, BAND_LEFT // BAND_TQ)
        qp = q_ref[0, pl.ds(q0, BAND_TQ), :]
        kt = k_ref[0, pl.ds(k0, BAND_WIN), :]
        vt = v_ref[0, pl.ds(k0, BAND_WIN), :]
        accs = []
        for a in range(2):
            s = _dot_nt(qp * hm_ref[0, a:a + 1, :], kt) + bias_ref[a, case]
            m = jnp.max(s, axis=-1, keepdims=True)
            pe = jnp.exp2(s - m).astype(BF16)
            accs.append(_dot(pe, _ones_outside(own[a], vt)))
        o = _normalize_pair(accs[0], accs[1], lane)
        o_ref[0, pl.ds(q0, BAND_TQ), :] = o.astype(o_ref.dtype)
        return 0

    lax.fori_loop(0, S // BAND_TQ, q_block, 0)


def _band_attn(q, k, v, hm, bias):
    B, S, _ = v.shape
    spec = pl.BlockSpec((1, S, PAIR_W), lambda p, b: (b, 0, p))
    return pl.pallas_call(
        functools.partial(_band_kernel, S),
        out_shape=jax.ShapeDtypeStruct((B, S, MIX_W), BF16),
        grid=(2, B),
        in_specs=[spec, spec, spec, pl.BlockSpec((1, 2, PAIR_W), lambda p, b: (p, 0, 0)),
                  pl.BlockSpec((2,) + bias.shape[1:], lambda p, b: (p, 0, 0, 0))],
        out_specs=spec,
        name="attn_band",
        compiler_params=_cparams(("arbitrary", "arbitrary")),
    )(q, k, v, hm, bias)


def _stick_kernel(S, q_ref, k_ref, v_ref, hm_ref, uo_ref, o_ref):
    T = ATT_BLOCK
    row = lax.broadcasted_iota(jnp.int32, (T, T), 0)
    col = lax.broadcasted_iota(jnp.int32, (T, T), 1)
    strict = col < row
    lane = lax.broadcasted_iota(jnp.int32, (T, PAIR_W), 1)

    def q_block(qi, _):
        q0 = pl.multiple_of(qi * T, T)
        qp = q_ref[0, pl.ds(q0, T), :]
        qm = [qp * hm_ref[0, a:a + 1, :] for a in range(2)]

        def step(kj, carry, diag):
            k0 = pl.multiple_of(kj * T, T)
            kt = k_ref[0, pl.ds(k0, T), :]
            vt = v_ref[0, pl.ds(k0, T), :]
            nrb = T // ROW_BLOCK
            chains = [(a, rb) for a in range(2) for rb in range(nrb)]
            rs = [slice(rb * ROW_BLOCK, (rb + 1) * ROW_BLOCK) for rb in range(nrb)]
            kw = [(rb + 1) * ROW_BLOCK if diag else T for rb in range(nrb)]
            z = [_dot_nt(qm[a][rs[rb]], kt[:kw[rb]]) for a, rb in chains]
            sp = []
            for (a, rb), zc in zip(chains, z):
                s_ = jnp.maximum(zc, 0.0) + jnp.log2(1.0 + jnp.exp2(-jnp.abs(zc)))
                if diag:
                    s_ = jnp.where(strict[rs[rb], :kw[rb]], s_, 0.0)
                sp.append(s_)
            cum = []
            for (a, rb), s_ in zip(chains, sp):
                sb = s_.astype(BF16)
                cum.append([_dot(sb[:, j * LANES:(j + 1) * LANES], uo_ref[...]) for j in range(kw[rb] // LANES)])
            w, runs = [], []
            for (a, rb), zc, s_, cm in zip(chains, z, sp, cum):
                right = [None] * len(cm)
                run = carry[a][1][rs[rb]]
                for j in reversed(range(len(cm))):
                    right[j] = run
                    run = run + cm[j][:, LANES:]
                w_ = jnp.exp2(zc - s_ - jnp.concatenate([c[:, :LANES] for c in cm], axis=1)
                              - jnp.concatenate(right, axis=1))
                if diag:
                    w_ = jnp.where(strict[rs[rb], :kw[rb]], w_, 0.0)
                w.append(w_.astype(BF16))
                runs.append(run)
            accs = [carry[a][0][rs[rb]] + _dot(w_, vt[:kw[rb]]) for (a, rb), w_ in zip(chains, w)]
            return tuple((jnp.concatenate(accs[a * nrb:(a + 1) * nrb], axis=0),
                          jnp.concatenate(runs[a * nrb:(a + 1) * nrb], axis=0)) for a in range(2))

        init = tuple((jnp.zeros((T, PAIR_W), F32), jnp.zeros((T, LANES), F32)) for _ in range(2))
        carry = step(qi, init, True)
        carry = lax.fori_loop(0, qi, lambda i, c: step(qi - 1 - i, c, False), carry)
        (a0, _), (a1, _) = carry
        o = jnp.where(lane < HEAD_DIM, a0, a1)
        o_ref[0, pl.ds(q0, T), :] = o.astype(o_ref.dtype)
        return 0

    lax.fori_loop(0, S // T, q_block, 0)


def _stick_attn(q, k, v, hm, uo):
    B, S, _ = v.shape
    spec = pl.BlockSpec((1, S, PAIR_W), lambda b, p: (b, 0, p))
    return pl.pallas_call(
        functools.partial(_stick_kernel, S),
        out_shape=jax.ShapeDtypeStruct((B, S, MIX_W), BF16),
        grid=(B, 2),
        in_specs=[spec, spec, spec, pl.BlockSpec((1, 2, PAIR_W), lambda b, p: (p, 0, 0)),
                  pl.BlockSpec(uo.shape, lambda b, p: (0, 0))],
        out_specs=spec,
        name="attn_stick",
        compiler_params=_cparams(("arbitrary", "arbitrary")),
    )(q, k, v, hm, uo)


def _route(scores_t, sel_t):
    G, E = N_GROUPS, EXPERTS_PER_GROUP
    sel = [sel_t[e:e + 1, :] for e in range(G * E)]
    sco = [scores_t[e:e + 1, :] for e in range(G * E)]
    gs = []
    for g in range(G):
        a, b, c, d = sel[E * g:E * g + E]
        gs.append(jnp.maximum(jnp.maximum(jnp.maximum(a + b, a + c), jnp.maximum(a + d, b + c)),
                              jnp.maximum(b + d, c + d)))
    gmax = jnp.maximum(jnp.maximum(gs[0], gs[1]), jnp.maximum(gs[2], gs[3]))
    grp = jnp.where(gs[0] == gmax, 0, jnp.where(gs[1] == gmax, 1, jnp.where(gs[2] == gmax, 2, 3)))

    def pick(rows, j):
        return jnp.where(grp == 0, rows[j], jnp.where(grp == 1, rows[E + j],
                                                      jnp.where(grp == 2, rows[2 * E + j], rows[3 * E + j])))

    ig = [pick(sel, j) for j in range(E)]
    igs = [pick(sco, j) for j in range(E)]

    def first_argmax(v):
        mx = jnp.maximum(jnp.maximum(v[0], v[1]), jnp.maximum(v[2], v[3]))
        return jnp.where(v[0] == mx, 0, jnp.where(v[1] == mx, 1, jnp.where(v[2] == mx, 2, 3)))

    l1 = first_argmax(ig)
    ig2 = [jnp.where(l1 == j, -jnp.inf, ig[j]) for j in range(E)]
    l2 = first_argmax(ig2)

    def take(v, idx):
        return jnp.where(idx == 0, v[0], jnp.where(idx == 1, v[1], jnp.where(idx == 2, v[2], v[3])))

    w1 = take(igs, l1)
    w2 = take(igs, l2)
    den = w1 + w2
    w1 = w1 / den
    w2 = w2 / den
    return grp, [jnp.where(l1 == j, w1, jnp.where(l2 == j, w2, 0.0)) for j in range(E)]


def _k5_kernel(x_ref, mod_ref, ng1_ref, ng2_ref, oa_ref, ob_ref, oc_ref, od_ref,
               wg_ref, wb_ref, wo_ref, wr_ref, rb_ref, x1_ref, h2_ref, cwp_ref, grp_ref):
    x = x_ref[0]
    D = x.shape[1]
    tm = x.shape[0]
    sh, sc, g_m = mod_ref[0, 0:1, :], mod_ref[0, 1:2, :], mod_ref[0, 2:3, :]
    sh_f, sc_f = mod_ref[0, 3:4, :], mod_ref[0, 4:5, :]
    hb = (_rms(x, ng1_ref[...]) * (1.0 + sc) + sh).astype(BF16)
    merged = None
    for b, o_ref in enumerate((oa_ref, ob_ref, oc_ref, od_ref)):
        gate = _sigmoid(_dot(hb, wg_ref[:, b * D:(b + 1) * D]))
        t = gate * _dot(o_ref[0], wb_ref[b])
        merged = t if merged is None else merged + t
    x1 = x + g_m * _dot(merged.astype(BF16), wo_ref[...])
    x1_ref[0] = x1
    h2 = (_rms(x1, ng2_ref[...]) * (1.0 + sc_f) + sh_f).astype(BF16)
    h2_ref[0] = h2
    scores = _sigmoid(_dot(h2, wr_ref[...]))
    st = scores.T[0:N_EXPERTS, :]
    grp, cw = _route(st, st + rb_ref[...])
    grp_ref[0] = grp
    nr = 16
    rid = lax.broadcasted_iota(jnp.int32, (nr, tm), 0)
    rec = jnp.where(rid == CW_GROUP_LANE, grp.astype(F32), 0.0)
    for j, c in enumerate(cw):
        hi = c.astype(BF16).astype(F32)
        mid = (c - hi).astype(BF16).astype(F32)
        for part, v in enumerate((hi, mid, c - hi - mid)):
            rec = jnp.where(rid == part * EXPERTS_PER_GROUP + j, v, rec)
    cwp_ref[0] = jnp.concatenate([rec, jnp.zeros((LANES - nr, tm), F32)], axis=0).T.astype(BF16)


def _k5(x, mod, ng1, ng2, oa, ob, oc, od, lw, wr, rb, tm):
    B, S, D = x.shape
    bs = lambda w: pl.BlockSpec((1, tm, w), lambda b, i: (b, i, 0))
    full = lambda a: pl.BlockSpec(a.shape, lambda b, i: (0,) * a.ndim)
    consts = (lw["wg"], lw["wb"], lw["wo"], wr, rb)
    return pl.pallas_call(
        _k5_kernel,
        out_shape=(jax.ShapeDtypeStruct((B, S, D), F32), jax.ShapeDtypeStruct((B, S, D), BF16),
                   jax.ShapeDtypeStruct((B, S, LANES), BF16), jax.ShapeDtypeStruct((B, 1, S), jnp.int32)),
        grid=(B, S // tm),
        in_specs=[bs(D), pl.BlockSpec((1, 6, D), lambda b, i: (b, 0, 0)), full(ng1), full(ng2),
                  bs(MIX_W), bs(MIX_W), bs(MIX_W), bs(MIX_W)] + [full(a) for a in consts],
        out_specs=(bs(D), bs(D), bs(LANES), pl.BlockSpec((1, 1, tm), lambda b, i: (b, 0, i))),
        name="merge_route",
        compiler_params=_cparams(("arbitrary", "arbitrary")),
    )(x, mod, ng1, ng2, oa, ob, oc, od, *consts)


def _swiglu_rows(load_h, nrows, wgu, wd, weight_fn, store_y):
    chunks = [slice(r, r + MOE_ROW_BLOCK) for r in range(0, nrows, MOE_ROW_BLOCK)]

    def up(rs):
        h = load_h(rs)
        return [_dot(h, w) for w in wgu]

    nxt = up(chunks[0])
    for i, rs in enumerate(chunks):
        gu = nxt
        if i + 1 < len(chunks):
            nxt = up(chunks[i + 1])
        acts = []
        for j, gu_j in enumerate(gu):
            g = gu_j[:, :D_EXPERT]
            a = g * _sigmoid(g) * gu_j[:, D_EXPERT:]
            w = weight_fn(rs, j)
            acts.append((a if w is None else a * w).astype(BF16))
        store_y(rs, _dot(acts[0] if len(acts) == 1 else jnp.concatenate(acts, axis=1), wd))


def _moe_plan(grp, n_tokens):
    ntd = n_tokens // MOE_TOKEN_TILE
    g = grp.reshape(ntd, MOE_TOKEN_TILE)
    cnt = jnp.sum((g[:, :, None] == jnp.arange(N_GROUPS, dtype=jnp.int32)).astype(jnp.int32), axis=1)
    c_al = (cnt + MOE_ALIGN - 1) // MOE_ALIGN * MOE_ALIGN
    c_ch = (cnt + MOE_CHUNK - 1) // MOE_CHUNK * MOE_CHUNK
    lo = jnp.cumsum(c_ch, axis=1) - c_ch
    tot = jnp.sum(c_al, axis=0)
    reg = (tot + MOE_CHUNK + MOE_EXPERT_TILE - 1) // MOE_EXPERT_TILE * MOE_EXPERT_TILE
    end = jnp.cumsum(reg)
    goff = (end - reg)[None, :] + jnp.cumsum(c_al, axis=0) - c_al
    start = jnp.arange(_moe_buffer_rows(n_tokens) // MOE_EXPERT_TILE, dtype=jnp.int32) * MOE_EXPERT_TILE
    tile_group = jnp.minimum(jnp.sum((start[:, None] >= end[None, :]).astype(jnp.int32), axis=1), N_GROUPS - 1)
    tile_valid = (start < end[-1]).astype(jnp.int32)
    flat = lambda a: a.reshape(-1).astype(jnp.int32)
    return flat(goff), flat(lo), flat(c_ch // MOE_CHUNK), tile_group.astype(jnp.int32), tile_valid


def _moe_buffer_rows(n_tokens):
    ntd = n_tokens // MOE_TOKEN_TILE
    worst = n_tokens + (MOE_ALIGN - 1) * N_GROUPS * ntd + N_GROUPS * (MOE_CHUNK + MOE_EXPERT_TILE)
    return (worst + MOE_EXPERT_TILE - 1) // MOE_EXPERT_TILE * MOE_EXPERT_TILE


def _segment_copies(i, goff_ref, lo_ref, nch_ref, make_copy, action):
    for g in range(N_GROUPS):
        lo_g = lo_ref[i * N_GROUPS + g]
        go_g = goff_ref[i * N_GROUPS + g]

        def body(c, _):
            action(make_copy(pl.multiple_of(lo_g + c * MOE_CHUNK, MOE_ALIGN),
                             pl.multiple_of(go_g + c * MOE_CHUNK, MOE_ALIGN)))
            return 0

        lax.fori_loop(0, nch_ref[i * N_GROUPS + g], body, 0)


def _dispatch_kernel(goff_ref, lo_ref, nch_ref, h2_ref, cwp_ref, grp_ref, ust_ref, xs_in_ref, xs_ref, xl_ref, sem):
    del xs_in_ref
    i = pl.program_id(0)
    T, R = MOE_TOKEN_TILE, MOE_LOCAL_ROWS
    gid = lax.broadcasted_iota(jnp.int32, (8, T), 0)
    onehot = jnp.where(grp_ref[0] == gid, 1.0, 0.0)
    rank = _dot(onehot.astype(BF16), ust_ref[...])
    lo_rows = jnp.zeros((8, T), F32)
    for g in range(N_GROUPS):
        lo_rows = jnp.where(gid == g, lo_ref[i * N_GROUPS + g].astype(F32), lo_rows)
    lp = jnp.sum(onehot * (rank + lo_rows), axis=0, keepdims=True).astype(jnp.int32)
    perm = jnp.where(lp == lax.broadcasted_iota(jnp.int32, (R, T), 0), 1.0, 0.0).astype(BF16)
    xl_ref[...] = _dot(perm, jnp.concatenate([h2_ref[...], cwp_ref[...]], axis=1)).astype(BF16)

    def make_copy(lo, go):
        return pltpu.make_async_copy(xl_ref.at[pl.ds(lo, MOE_CHUNK)], xs_ref.at[pl.ds(go, MOE_CHUNK)], sem)

    _segment_copies(i, goff_ref, lo_ref, nch_ref, make_copy, lambda cp: cp.start())
    _segment_copies(i, goff_ref, lo_ref, nch_ref, make_copy, lambda cp: cp.wait())


def _dispatch(plan, h2, cwp, grp, ust):
    N, D = h2.shape
    goff, lo, nch, _, _ = plan
    T = MOE_TOKEN_TILE
    rows = _moe_buffer_rows(N)
    xs0 = jnp.zeros((rows, D + LANES), BF16)
    grid_spec = pltpu.PrefetchScalarGridSpec(
        num_scalar_prefetch=3,
        grid=(N // T,),
        in_specs=[pl.BlockSpec((T, D), lambda i, *_: (i, 0)),
                  pl.BlockSpec((T, LANES), lambda i, *_: (i, 0)),
                  pl.BlockSpec((1, 1, T), lambda i, *_: (i, 0, 0)),
                  pl.BlockSpec((T, T), lambda i, *_: (0, 0)),
                  pl.BlockSpec(memory_space=pl.ANY)],
        out_specs=pl.BlockSpec(memory_space=pl.ANY),
        scratch_shapes=[pltpu.VMEM((MOE_LOCAL_ROWS, D + LANES), BF16), pltpu.SemaphoreType.DMA(())],
    )
    return pl.pallas_call(
        _dispatch_kernel,
        out_shape=jax.ShapeDtypeStruct((rows, D + LANES), BF16),
        grid_spec=grid_spec,
        input_output_aliases={7: 0},
        name="moe_dispatch",
        compiler_params=_cparams(("arbitrary",)),
    )(goff, lo, nch, h2, cwp, grp.reshape(N // T, 1, T), ust, xs0)


def _expert_kernel(tg_ref, tv_ref, xs_ref, wgu_ref, wd_ref, ys_ref):
    del tg_ref
    i = pl.program_id(0)
    D = ys_ref.shape[1]
    E = EXPERTS_PER_GROUP
    lane = lax.broadcasted_iota(jnp.int32, (MOE_ROW_BLOCK, LANES), 1)

    @pl.when(tv_ref[i] != 0)
    def _():
        def weight(rs, j):
            rec = xs_ref[rs, D:].astype(F32)
            return jnp.sum(jnp.where((lane % E == j) & (lane < 3 * E), rec, 0.0), axis=-1, keepdims=True)

        def store(rs, y):
            ys_ref[rs, :] = y.astype(ys_ref.dtype)

        _swiglu_rows(lambda rs: xs_ref[rs, :D], MOE_EXPERT_TILE, [wgu_ref[0, j] for j in range(E)],
                     wd_ref[0], weight, store)

    @pl.when(tv_ref[i] == 0)
    def _():
        ys_ref[...] = jnp.zeros_like(ys_ref)


def _experts(plan, xs, wgu, wd):
    rows, XW = xs.shape
    D = XW - LANES
    _, _, _, tile_group, tile_valid = plan
    E = EXPERTS_PER_GROUP
    TE = MOE_EXPERT_TILE
    grid_spec = pltpu.PrefetchScalarGridSpec(
        num_scalar_prefetch=2,
        grid=(rows // TE,),
        in_specs=[pl.BlockSpec((TE, XW), lambda i, tg, tv: (i, 0)),
                  pl.BlockSpec((1, E, D, 2 * D_EXPERT), lambda i, tg, tv: (tg[i], 0, 0, 0)),
                  pl.BlockSpec((1, E * D_EXPERT, D), lambda i, tg, tv: (tg[i], 0, 0))],
        out_specs=pl.BlockSpec((TE, D), lambda i, tg, tv: (i, 0)),
    )
    return pl.pallas_call(
        _expert_kernel,
        out_shape=jax.ShapeDtypeStruct((rows, D), BF16),
        grid_spec=grid_spec,
        name="moe_experts",
        compiler_params=_cparams(("arbitrary",)),
    )(tile_group, tile_valid, xs, wgu, wd)


def _combine_kernel(goff_ref, lo_ref, nch_ref, ys_ref, x1_ref, h2_ref, cwp_ref, mod_ref, lst_ref,
                    wgu_s_ref, wd_s_ref, out_ref, yl_ref, sem):
    i = pl.program_id(0)
    n = pl.num_programs(0)
    T, R = MOE_TOKEN_TILE, MOE_LOCAL_ROWS

    def make_copy(slot):
        return lambda lo, go: pltpu.make_async_copy(
            ys_ref.at[pl.ds(go, MOE_CHUNK)], yl_ref.at[slot, pl.ds(lo, MOE_CHUNK)], sem.at[slot])

    def fetch(t, slot):
        _segment_copies(t, goff_ref, lo_ref, nch_ref, make_copy(slot), lambda cp: cp.start())

    slot = i % 2

    @pl.when(i == 0)
    def _():
        fetch(0, 0)

    @pl.when(i + 1 < n)
    def _():
        fetch(i + 1, 1 - slot)

    lane = lax.broadcasted_iota(jnp.int32, (T, LANES), 1)
    gcol = cwp_ref[:, CW_GROUP_LANE:CW_GROUP_LANE + 1].astype(F32).astype(jnp.int32)
    onehot = jnp.where(gcol == lane, 1.0, 0.0)
    rank = _dot(lst_ref[...], onehot.astype(BF16))
    lo_lanes = jnp.zeros((T, LANES), F32)
    for g in range(N_GROUPS):
        lo_lanes = jnp.where(lane == g, lo_ref[i * N_GROUPS + g].astype(F32), lo_lanes)
    lp = jnp.sum(onehot * (rank + lo_lanes), axis=-1, keepdims=True).astype(jnp.int32)
    perm_t = jnp.where(lp == lax.broadcasted_iota(jnp.int32, (T, R), 1), 1.0, 0.0).astype(BF16)

    acc = []
    _swiglu_rows(lambda rs: h2_ref[rs, :], T, [wgu_s_ref[...]], wd_s_ref[...], lambda rs, j: None,
                 lambda rs, y: acc.append(y))
    shared = acc[0] if len(acc) == 1 else jnp.concatenate(acc, axis=0)

    _segment_copies(i, goff_ref, lo_ref, nch_ref, make_copy(slot), lambda cp: cp.wait())
    used = lo_ref[i * N_GROUPS + N_GROUPS - 1] + nch_ref[i * N_GROUPS + N_GROUPS - 1] * MOE_CHUNK
    row = lax.broadcasted_iota(jnp.int32, (R, out_ref.shape[1]), 0)
    yl = jnp.where(row < used, yl_ref[slot], jnp.zeros((), yl_ref.dtype))
    out_ref[...] = x1_ref[...] + mod_ref[0, 5:6, :] * (shared + _dot(perm_t, yl))


def _combine(plan, ys, x1, h2, cwp, mod, lst, wgu_s, wd_s, S):
    N, D = h2.shape
    goff, lo, nch, _, _ = plan
    T = MOE_TOKEN_TILE
    per_b = S // T
    row = lambda w: pl.BlockSpec((T, w), lambda i, *_: (i, 0))
    grid_spec = pltpu.PrefetchScalarGridSpec(
        num_scalar_prefetch=3,
        grid=(N // T,),
        in_specs=[pl.BlockSpec(memory_space=pl.ANY), row(D), row(D), row(LANES),
                  pl.BlockSpec((1, 6, D), lambda i, *_: (i // per_b, 0, 0)),
                  pl.BlockSpec((T, T), lambda i, *_: (0, 0)),
                  pl.BlockSpec((D, 2 * D_EXPERT), lambda i, *_: (0, 0)),
                  pl.BlockSpec((D_EXPERT, D), lambda i, *_: (0, 0))],
        out_specs=row(D),
        scratch_shapes=[pltpu.VMEM((2, MOE_LOCAL_ROWS, D), BF16), pltpu.SemaphoreType.DMA((2,))],
    )
    return pl.pallas_call(
        _combine_kernel,
        out_shape=jax.ShapeDtypeStruct((N, D), F32),
        grid_spec=grid_spec,
        name="moe_combine",
        compiler_params=_cparams(("arbitrary",)),
    )(goff, lo, nch, ys, x1, h2, cwp, mod, lst, wgu_s, wd_s)


def _const_tables():
    lane = np.arange(MIX_W)
    bd = (lane[:, None] // HEAD_DIM == lane[None, :] // HEAD_DIM).astype(np.float32)
    mp = np.zeros((2, MIX_W, MIX_W), np.float32)
    hm_c = np.zeros((2, 2, MIX_W), np.float32)
    for p in range(2):
        head = np.where(lane < LANES, 2 * p + lane // HEAD_DIM, ((lane - LANES) % HEAD_DIM) // C_HALF)
        mp[p] = head[:, None] == head[None, :]
        for a in range(2):
            hm_c[p, a] = head == 2 * p + a
    hm = np.zeros((2, 2, PAIR_W), np.float32)
    for a in range(2):
        hm[:, a] = (np.arange(PAIR_W) // HEAD_DIM == a)
    j = np.arange(LANES)
    ustrict = (j[:, None] > j[None, :]).astype(np.float32)
    uo = np.concatenate([ustrict, np.ones((LANES, LANES), np.float32)], axis=1)
    t = np.arange(MOE_TOKEN_TILE)
    earlier = (t[:, None] < t[None, :]).astype(np.float32)
    return (jnp.asarray(bd, BF16), jnp.asarray(mp, BF16), jnp.asarray(hm, BF16),
            jnp.asarray(hm_c, BF16), jnp.asarray(uo, BF16), jnp.asarray(earlier, BF16),
            jnp.asarray(earlier.T, BF16))


def _band_bias(rel_bias):
    L, H, _ = rel_bias.shape
    R, Wn = BAND_TQ, BAND_WIN
    ncase = BAND_LEFT // BAND_TQ + 1
    off = (np.arange(ncase) * BAND_TQ)[:, None]
    n = R + Wn
    k = np.arange(n)[None, :]
    dist = np.where(k <= Wn, off - k, off + n - k)
    idx = np.clip(dist, -D_MAX_REL, D_MAX_REL) + D_MAX_REL
    u = rel_bias[:, :, jnp.asarray(idx)]
    bias = jnp.tile(u, (1, 1, 1, R))[..., :R * (n - 1)].reshape(L, H, ncase, R, n - 1)[..., :Wn]
    r = np.arange(R)[None, :, None]
    c = np.arange(Wn)[None, None, :]
    kc = np.floor_divide(c - off[:, :, None], CHUNK)
    rc = r // CHUNK
    vis = (kc <= rc) & (kc >= rc - D_LEFT_CHUNKS)
    return jnp.where(jnp.asarray(vis)[None, None], bias * LOG2E, NEG_INF).astype(F32)


def _pack_weights(w_in, fox_forget_b, fox_q_g, fox_k_g, mla_cq_g, mla_ckv_g, mla_w_uq, mla_w_ukv,
                  mla_q_g, mla_k_g, chunk_q_g, chunk_k_g, w_branch, w_out):
    L, D, _ = w_in.shape
    W = MIX_W
    a0, b0 = 0, 3 * W + N_HEADS
    c0 = b0 + 3 * W
    d0 = c0 + C_Q_RANK + C_KV_RANK + C_ROPE
    g0 = d0 + 3 * W
    scale = HEAD_DIM ** -0.5
    kr = w_in[:, :, c0 + C_Q_RANK + C_KV_RANK:d0]
    w1 = jnp.concatenate([
        w_in[:, :, a0:a0 + 3 * W],
        w_in[:, :, b0:b0 + W] * (scale * LOG2E), w_in[:, :, b0 + W:b0 + 3 * W],
        w_in[:, :, d0:d0 + 3 * W],
        w_in[:, :, c0:c0 + C_Q_RANK + C_KV_RANK],
        jnp.tile(kr[:, :, :C_HALF], (1, 1, N_HEADS)), jnp.tile(kr[:, :, C_HALF:], (1, 1, N_HEADS)),
    ], axis=2).astype(BF16)
    wfa = jnp.pad(jnp.swapaxes(w_in[:, :, 3 * W:3 * W + N_HEADS], 1, 2), ((0, 0), (0, 8 - N_HEADS), (0, 0))).astype(BF16)
    fb = jnp.pad(fox_forget_b, ((0, 0), (0, 8 - N_HEADS))).reshape(L, 8, 1).astype(F32)
    wg = w_in[:, :, g0:].astype(BF16)

    uq = mla_w_uq.reshape(L, C_Q_RANK, N_HEADS, C_QK)
    uq_rope = jnp.concatenate([uq[..., C_NOPE:C_NOPE + C_HALF].reshape(L, C_Q_RANK, N_HEADS * C_HALF),
                               uq[..., C_NOPE + C_HALF:].reshape(L, C_Q_RANK, N_HEADS * C_HALF)], axis=2)
    wuq = jnp.concatenate([uq[:, :, 0, :C_NOPE], uq[:, :, 1, :C_NOPE], uq_rope,
                           uq[:, :, 2, :C_NOPE], uq[:, :, 3, :C_NOPE], uq_rope], axis=2).astype(BF16)
    ukv = mla_w_ukv.reshape(L, C_KV_RANK, N_HEADS, C_NOPE + HEAD_DIM)
    wukv = jnp.concatenate([ukv[..., :C_NOPE].reshape(L, C_KV_RANK, W),
                            ukv[..., C_NOPE:].reshape(L, C_KV_RANK, W)], axis=2).astype(BF16)

    def c_gain(g, s):
        nope = g[:, :C_NOPE]
        rope = jnp.concatenate([jnp.tile(g[:, C_NOPE:C_NOPE + C_HALF], (1, N_HEADS)),
                                jnp.tile(g[:, C_NOPE + C_HALF:], (1, N_HEADS))], axis=1)
        pair = jnp.concatenate([nope, nope, rope], axis=1)
        return jnp.concatenate([pair, pair], axis=1) * s

    def pad512(v):
        return jnp.pad(v, ((0, 0), (0, 2 * W - v.shape[1])))

    gv = jnp.stack([
        pad512(jnp.tile(fox_q_g, (1, N_HEADS)) * (scale * LOG2E)), pad512(jnp.tile(fox_k_g, (1, N_HEADS))),
        pad512(jnp.tile(chunk_q_g, (1, N_HEADS)) * (scale * LOG2E)), pad512(jnp.tile(chunk_k_g, (1, N_HEADS))),
        pad512(mla_cq_g), pad512(mla_ckv_g), c_gain(mla_q_g, C_QK ** -0.5 * LOG2E), c_gain(mla_k_g, 1.0),
    ], axis=1).astype(F32)
    return dict(w1=w1, wfa=wfa, fb=fb, wg=wg, wuq=wuq, wukv=wukv, gv=gv,
                wb=w_branch.astype(BF16), wo=w_out.astype(BF16))


def kernel(x, c, positions, norm_mix_g, norm_ffn_g, w_ada, b_ada, w_in, fox_forget_b, fox_q_g, fox_k_g, mla_cq_g, mla_ckv_g, mla_w_uq, mla_w_ukv, mla_q_g, mla_k_g, chunk_q_g, chunk_k_g, chunk_rel_bias, w_branch, w_out, router_w, router_b, exp_w_gate, exp_w_up, exp_w_down, sh_w_gate, sh_w_up, sh_w_down):
    B, S, D = x.shape
    L = w_in.shape[0]
    assert S % ATT_BLOCK == 0 and S >= BAND_WIN and D % LANES == 0
    tm_p1 = min(S, 512)
    tm_k5 = min(S, 256)

    bd, mp, hm, hm_c, uo, ust, lst = _const_tables()
    pw = _pack_weights(w_in, fox_forget_b, fox_q_g, fox_k_g, mla_cq_g, mla_ckv_g, mla_w_uq, mla_w_ukv,
                       mla_q_g, mla_k_g, chunk_q_g, chunk_k_g, w_branch, w_out)
    band_bias = _band_bias(chunk_rel_bias)
    wr = jnp.pad(router_w, ((0, 0), (0, LANES - N_EXPERTS))).astype(BF16)
    rb = router_b.reshape(N_EXPERTS, 1).astype(F32)
    wgu = jnp.concatenate([exp_w_gate, exp_w_up], axis=-1).astype(BF16).reshape(
        L, N_GROUPS, EXPERTS_PER_GROUP, D, 2 * D_EXPERT)
    wd = exp_w_down.astype(BF16).reshape(L, N_GROUPS, EXPERTS_PER_GROUP * D_EXPERT, D)
    wgu_s = jnp.concatenate([sh_w_gate, sh_w_up], axis=-1).astype(BF16)
    wd_s = sh_w_down.astype(BF16)

    mod = _modulation(c, w_ada, b_ada).reshape(L, B, 6, D)
    cosl, sinl = _rope_tables(positions)

    for l in range(L):
        lw = {k: v[l] for k, v in pw.items()}
        lw["bd"], lw["mp"] = bd, mp
        ng1 = norm_mix_g[l].reshape(1, D)
        ng2 = norm_ffn_g[l].reshape(1, D)
        (qa, ka, va, lf, qb, kb, vb, qc, kc, vc, qd, kd, vd) = _p1(x, mod[l], ng1, lw, cosl, sinl, tm_p1)
        oa = _softmax_attn("fox", qa, ka, va, hm, lf)
        ob = _stick_attn(qb, kb, vb, hm, uo)
        oc = _softmax_attn("mla", qc, kc, vc, hm_c)
        od = _band_attn(qd, kd, vd, hm, band_bias[l])
        x1, h2, cwp, grp = _k5(x, mod[l], ng1, ng2, oa, ob, oc, od, lw, wr, rb, tm_k5)
        h2, cwp = h2.reshape(B * S, D), cwp.reshape(B * S, LANES)
        plan = _moe_plan(grp, B * S)
        xs = _dispatch(plan, h2, cwp, grp, ust)
        ys = _experts(plan, xs, wgu[l], wd[l])
        x = _combine(plan, ys, x1.reshape(B * S, D), h2, cwp, mod[l], lst, wgu_s[l], wd_s[l], S).reshape(B, S, D)
    return x
```

```python
import functools
import math

import numpy as np
import jax
import jax.numpy as jnp
from jax import lax
from jax.experimental import pallas as pl
from jax.experimental.pallas import tpu as pltpu

F32 = jnp.float32
BF16 = jnp.bfloat16

HEAD_DIM = 64
N_HEADS = 4
MIX_W = N_HEADS * HEAD_DIM
CHUNK = 64
NEG_INF = -1e30
EPS = 1e-6
C_Q_RANK = 256
C_KV_RANK = 128
C_NOPE = 64
C_ROPE = 32
C_HALF = C_ROPE // 2
C_QK = C_NOPE + C_ROPE
ROPE_BASE = 10000.0
D_LEFT_CHUNKS = 8
D_MAX_REL = 128
N_BRANCH = 4
N_EXPERTS = 16
N_GROUPS = 4
EXPERTS_PER_GROUP = 4
D_EXPERT = 256
LOG2E = math.log2(math.e)

LANES = 128
ATT_BLOCK = 512
ROW_BLOCK = 256
SOFTMAX_ROW_BLOCK = 512
MERGE_ROW_BLOCK = 256
MOE_ROW_BLOCK = 256
MOE_TOKEN_TILE = 256
MOE_ALIGN = 16
MOE_CHUNK = 32
MOE_LOCAL_ROWS = MOE_TOKEN_TILE + N_GROUPS * MOE_CHUNK
MOE_EXPERT_TILE = 512
CW_GROUP_LANE = 3 * EXPERTS_PER_GROUP
BAND_TQ = 256
BAND_LEFT = D_LEFT_CHUNKS * CHUNK
BAND_WIN = BAND_LEFT + BAND_TQ
PAIR_W = 2 * HEAD_DIM
VMEM_LIMIT = 56 * 1024 * 1024

P1_COLS = 9 * MIX_W + C_Q_RANK + C_KV_RANK + LANES


def _cparams(sem):
    return pltpu.CompilerParams(dimension_semantics=sem, vmem_limit_bytes=VMEM_LIMIT)


def _dot(a, b):
    return jnp.dot(a, b, preferred_element_type=F32)


def _dot_nt(a, b):
    return lax.dot_general(a, b, (((1,), (1,)), ((), ())), preferred_element_type=F32)


def _sigmoid(x):
    return 1.0 / (1.0 + jnp.exp(-x))


def _rms(x, g):
    return x * lax.rsqrt(jnp.mean(x * x, axis=-1, keepdims=True) + EPS) * g


def _mod_kernel(c_ref, w_ref, b_ref, o_ref):
    c = c_ref[...]
    cond = c * _sigmoid(c)
    o_ref[0] = jnp.dot(cond, w_ref[0], preferred_element_type=F32,
                       precision=lax.Precision.HIGHEST) + b_ref[0]


def _modulation(c, w_ada, b_ada):
    L, D, D6 = w_ada.shape
    B = c.shape[0]
    tn = 1024
    return pl.pallas_call(
        _mod_kernel,
        out_shape=jax.ShapeDtypeStruct((L, B, D6), F32),
        grid=(L, D6 // tn),
        in_specs=[pl.BlockSpec((B, D), lambda l, j: (0, 0)),
                  pl.BlockSpec((1, D, tn), lambda l, j: (l, 0, j)),
                  pl.BlockSpec((1, 1, tn), lambda l, j: (l, 0, j))],
        out_specs=pl.BlockSpec((1, B, tn), lambda l, j: (l, 0, j)),
        name="adaln_mod",
        compiler_params=_cparams(("arbitrary", "arbitrary")),
    )(c, w_ada, b_ada.reshape(L, 1, D6))


def _rope_kernel(ang_ref, sign_ref, cos_ref, sin_ref):
    a = ang_ref[0]
    cos_ref[0] = jnp.cos(a)
    sin_ref[0] = jnp.sin(a) * sign_ref[...]


def _rope_tables(positions):
    B, S = positions.shape
    inv_freq = jnp.power(ROPE_BASE, -jnp.arange(C_HALF, dtype=F32) / C_HALF)
    inv_l = jnp.tile(inv_freq, LANES // C_HALF)
    ang = positions.astype(F32)[:, :, None] * inv_l[None, None, :]
    sign = jnp.where(jnp.arange(LANES) < LANES // 2, -1.0, 1.0).astype(F32).reshape(1, LANES)
    ts = min(S, 1024)
    spec = pl.BlockSpec((1, ts, LANES), lambda b, i: (b, i, 0))
    return pl.pallas_call(
        _rope_kernel,
        out_shape=(jax.ShapeDtypeStruct((B, S, LANES), F32),) * 2,
        grid=(B, S // ts),
        in_specs=[spec, pl.BlockSpec((1, LANES), lambda b, i: (0, 0))],
        out_specs=(spec, spec),
        name="rope_tables",
        compiler_params=_cparams(("arbitrary", "arbitrary")),
    )(ang, sign)


def _rope(t, cos, sin):
    return t * cos + pltpu.roll(t, LANES // 2, 1) * sin


def _p1_kernel(x_ref, mod_ref, ng_ref, w1_ref, wfa_ref, fb_ref, gv_ref, wuq_ref, wukv_ref,
               bd_ref, mp_ref, cos_ref, sin_ref,
               qa_ref, ka_ref, va_ref, lf_ref, qb_ref, kb_ref, vb_ref,
               qc_ref, kc_ref, vc_ref, qd_ref, kd_ref, vd_ref):
    x = x_ref[0]
    sh = mod_ref[0, 0:1, :]
    sc = mod_ref[0, 1:2, :]
    hb = (_rms(x, ng_ref[...]) * (1.0 + sc) + sh).astype(BF16)
    W = MIX_W

    def mm(c0, c1):
        return _dot(hb, w1_ref[:, c0:c1])

    def head_norm(t, g):
        ss = _dot((t * t).astype(BF16), bd_ref[...])
        return (t * lax.rsqrt(ss * (1.0 / HEAD_DIM) + EPS) * g).astype(BF16)

    r_a = mm(0, 3 * W)
    fa = _dot_nt(wfa_ref[...], hb) + fb_ref[...]
    r_b = mm(3 * W, 6 * W)

    qa_ref[0] = head_norm(r_a[:, 0:W], gv_ref[0:1, 0:W])
    ka_ref[0] = head_norm(r_a[:, W:2 * W], gv_ref[1:2, 0:W])
    va_ref[0] = r_a[:, 2 * W:3 * W].astype(BF16)
    lf_ref[0] = jnp.minimum(fa, 0.0) - jnp.log(1.0 + jnp.exp(-jnp.abs(fa)))
    r_d = mm(6 * W, 9 * W)

    qb_ref[0] = r_b[:, 0:W].astype(BF16)
    kb_ref[0] = r_b[:, W:2 * W].astype(BF16)
    vb_ref[0] = r_b[:, 2 * W:3 * W].astype(BF16)
    c0 = 9 * W
    r = mm(c0, c0 + C_Q_RANK + C_KV_RANK + LANES)

    qd_ref[0] = head_norm(r_d[:, 0:W], gv_ref[2:3, 0:W])
    kd_ref[0] = head_norm(r_d[:, W:2 * W], gv_ref[3:4, 0:W])
    vd_ref[0] = r_d[:, 2 * W:3 * W].astype(BF16)

    cos = cos_ref[0]
    sin = sin_ref[0]
    cq = _rms(r[:, 0:C_Q_RANK], gv_ref[4:5, 0:C_Q_RANK]).astype(BF16)
    ckv = _rms(r[:, C_Q_RANK:C_Q_RANK + C_KV_RANK], gv_ref[5:6, 0:C_KV_RANK]).astype(BF16)
    kr = _rope(r[:, C_Q_RANK + C_KV_RANK:], cos, sin)
    qu = _dot(cq, wuq_ref[...])
    kvu = _dot(ckv, wukv_ref[...])
    vc_ref[0] = kvu[:, 2 * LANES:].astype(BF16)
    for p in range(2):
        qp = jnp.concatenate([qu[:, p * 2 * LANES:p * 2 * LANES + LANES],
                              _rope(qu[:, p * 2 * LANES + LANES:(p + 1) * 2 * LANES], cos, sin)], axis=1)
        kp = jnp.concatenate([kvu[:, p * LANES:(p + 1) * LANES], kr], axis=1)
        for t, gr, dst in ((qp, 6, qc_ref), (kp, 7, kc_ref)):
            ss = _dot((t * t).astype(BF16), mp_ref[p])
            tn = t * lax.rsqrt(ss * (1.0 / C_QK) + EPS) * gv_ref[gr:gr + 1, p * 2 * LANES:(p + 1) * 2 * LANES]
            dst[0, :, p * 2 * LANES:(p + 1) * 2 * LANES] = tn.astype(BF16)


def _p1(x, mod, ng, lw, cosl, sinl, tm):
    B, S, D = x.shape
    W = MIX_W
    bs = lambda w: pl.BlockSpec((1, tm, w), lambda b, i: (b, i, 0))
    full = lambda a: pl.BlockSpec(a.shape, lambda b, i: (0,) * a.ndim)
    o256 = jax.ShapeDtypeStruct((B, S, W), BF16)
    o512 = jax.ShapeDtypeStruct((B, S, 2 * W), BF16)
    out_shape = (o256, o256, o256, jax.ShapeDtypeStruct((B, 8, S), F32),
                 o256, o256, o256, o512, o512, o256, o256, o256, o256)
    out_specs = (bs(W), bs(W), bs(W), pl.BlockSpec((1, 8, tm), lambda b, i: (b, 0, i)),
                 bs(W), bs(W), bs(W), bs(2 * W), bs(2 * W), bs(W), bs(W), bs(W), bs(W))
    consts = (ng, lw["w1"], lw["wfa"], lw["fb"], lw["gv"], lw["wuq"], lw["wukv"], lw["bd"], lw["mp"])
    return pl.pallas_call(
        _p1_kernel,
        out_shape=out_shape,
        grid=(B, S // tm),
        in_specs=[bs(D), pl.BlockSpec((1, 6, D), lambda b, i: (b, 0, 0))]
                 + [full(a) for a in consts] + [bs(LANES), bs(LANES)],
        out_specs=out_specs,
        name="in_proj",
        compiler_params=_cparams(("arbitrary", "arbitrary")),
    )(x, mod, *consts, cosl, sinl)


def _lane_cumsum(x):
    n = x.shape[1]
    idx = lax.broadcasted_iota(jnp.int32, x.shape, 1)
    k = 1
    while k < n:
        x = x + jnp.where(idx >= k, pltpu.roll(x, k, 1), 0.0)
        k *= 2
    return x


def _ones_outside(own, v):
    return jnp.where(own[:v.shape[0]], v, jnp.ones_like(v))


def _normalize_pair(acc0, acc1, lane):
    half = PAIR_W // 2
    o0 = acc0 / pltpu.roll(acc0, half, 1)
    o1 = acc1 / pltpu.roll(acc1, half, 1)
    return jnp.where(lane < HEAD_DIM, o0, o1)


def _softmax_attn_kernel(mode, S, *refs):
    T = ATT_BLOCK
    if mode == "fox":
        q_ref, k_ref, v_ref, hm_ref, lf_ref, o_ref, cf_ref = refs
        cf = _lane_cumsum(lf_ref[0]) * LOG2E
        for h in range(N_HEADS):
            cf_ref[h] = cf[h:h + 1, :]
    else:
        q_ref, k_ref, v_ref, hm_ref, o_ref = refs
    cw = q_ref.shape[2] // 2
    row = lax.broadcasted_iota(jnp.int32, (T, T), 0)
    col = lax.broadcasted_iota(jnp.int32, (T, T), 1)
    if mode == "fox":
        dmask = col <= row
    else:
        dmask = (col // CHUNK) <= (row // CHUNK)
    lane = lax.broadcasted_iota(jnp.int32, (T, PAIR_W), 1)
    own = [lane // HEAD_DIM == a for a in range(2)]
    heads = [(p, a) for p in range(2) for a in range(2)]

    def q_block(qi, _):
        q0 = pl.multiple_of(qi * T, T)
        qm = [q_ref[0, pl.ds(q0, T), p * cw:(p + 1) * cw] * hm_ref[p, a:a + 1, :] for p, a in heads]

        def step(kj, carry, diag):
            k0 = pl.multiple_of(kj * T, T)
            kt = [k_ref[0, pl.ds(k0, T), p * cw:(p + 1) * cw] for p in range(2)]
            vt = [v_ref[0, pl.ds(k0, T), p * PAIR_W:(p + 1) * PAIR_W] for p in range(2)]
            s = [_dot_nt(qm[h], kt[p]) for h, (p, a) in enumerate(heads)]
            if mode == "fox":
                s = [s[h] - cf_ref[h, :, pl.ds(k0, T)] for h in range(N_HEADS)]
            if diag:
                s = [jnp.where(dmask, sc, NEG_INF) for sc in s]
            m_new = [jnp.maximum(c[0], jnp.max(sc, axis=-1, keepdims=True)) for c, sc in zip(carry, s)]
            pe = [jnp.exp2(sc - mn).astype(BF16) for sc, mn in zip(s, m_new)]
            acc = [jnp.exp2(c[0] - mn) * c[1] + _dot(p_, _ones_outside(own[a], vt[p]))
                   for (p, a), c, mn, p_ in zip(heads, carry, m_new, pe)]
            return tuple(zip(m_new, acc))

        init = tuple((jnp.full((T, 1), NEG_INF, F32), jnp.zeros((T, PAIR_W), F32)) for _ in range(N_HEADS))
        carry = lax.fori_loop(0, qi, lambda kj, c: step(kj, c, False), init)
        carry = step(qi, carry, True)
        for p in range(2):
            o = _normalize_pair(carry[2 * p][1], carry[2 * p + 1][1], lane)
            o_ref[0, pl.ds(q0, T), p * PAIR_W:(p + 1) * PAIR_W] = o.astype(o_ref.dtype)
        return 0

    lax.fori_loop(0, S // T, q_block, 0)


def _softmax_attn(mode, q, k, v, hm, lf=None):
    B, S, _ = v.shape
    spec = lambda w: pl.BlockSpec((1, S, w), lambda b: (b, 0, 0))
    in_specs = [spec(q.shape[2]), spec(q.shape[2]), spec(MIX_W), pl.BlockSpec(hm.shape, lambda b: (0, 0, 0))]
    args = [q, k, v, hm]
    scratch = []
    if mode == "fox":
        in_specs.append(pl.BlockSpec((1, 8, S), lambda b: (b, 0, 0)))
        args.append(lf)
        scratch = [pltpu.VMEM((N_HEADS, 1, S), F32)]
    return pl.pallas_call(
        functools.partial(_softmax_attn_kernel, mode, S),
        out_shape=jax.ShapeDtypeStruct((B, S, MIX_W), BF16),
        grid=(B,),
        in_specs=in_specs,
        out_specs=spec(MIX_W),
        scratch_shapes=scratch,
        name="attn_" + mode,
        compiler_params=_cparams(("arbitrary",)),
    )(*args)


def _band_kernel(S, q_ref, k_ref, v_ref, hm_ref, bias_ref, o_ref):
    lane = lax.broadcasted_iota(jnp.int32, (BAND_TQ, PAIR_W), 1)
    vlane = lax.broadcasted_iota(jnp.int32, (BAND_WIN, PAIR_W), 1)
    own = [vlane // HEAD_DIM == a for a in range(2)]

    def q_block(qi, _):
        q0 = pl.multiple_of(qi * BAND_TQ, BAND_TQ)
        k0 = pl.multiple_of(jnp.maximum(q0 - BAND_LEFT, 0), BAND_TQ)
        case = jnp.minimum(qi, BAND_LEFT // BAND_TQ)
        heads = [(p, a) for p in range(2) for a in range(2)]
        ps = [slice(p * PAIR_W, (p + 1) * PAIR_W) for p in range(2)]
        kt = [k_ref[0, pl.ds(k0, BAND_WIN), ps[p]] for p in range(2)]
        vt = [v_ref[0, pl.ds(k0, BAND_WIN), ps[p]] for p in range(2)]
        s = [_dot_nt(q_ref[0, pl.ds(q0, BAND_TQ), ps[p]] * hm_ref[p, a:a + 1, :], kt[p])
             + bias_ref[2 * p + a, case] for p, a in heads]
        pe = [jnp.exp2(sc - jnp.max(sc, axis=-1, keepdims=True)).astype(BF16) for sc in s]
        accs = [_dot(p_, _ones_outside(own[a], vt[p])) for (p, a), p_ in zip(heads, pe)]
        for p in range(2):
            o = _normalize_pair(accs[2 * p], accs[2 * p + 1], lane)
            o_ref[0, pl.ds(q0, BAND_TQ), ps[p]] = o.astype(o_ref.dtype)
        return 0

    lax.fori_loop(0, S // BAND_TQ, q_block, 0)


def _band_attn(q, k, v, hm, bias):
    B, S, _ = v.shape
    spec = pl.BlockSpec((1, S, MIX_W), lambda b: (b, 0, 0))
    return pl.pallas_call(
        functools.partial(_band_kernel, S),
        out_shape=jax.ShapeDtypeStruct((B, S, MIX_W), BF16),
        grid=(B,),
        in_specs=[spec, spec, spec, pl.BlockSpec(hm.shape, lambda b: (0, 0, 0)),
                  pl.BlockSpec(bias.shape, lambda b: (0, 0, 0, 0))],
        out_specs=spec,
        name="attn_band",
        compiler_params=_cparams(("arbitrary",)),
    )(q, k, v, hm, bias)


def _stick_kernel(S, q_ref, k_ref, v_ref, hm_ref, uo_ref, o_ref):
    T = ATT_BLOCK
    row = lax.broadcasted_iota(jnp.int32, (T, T), 0)
    col = lax.broadcasted_iota(jnp.int32, (T, T), 1)
    strict = col < row
    lane = lax.broadcasted_iota(jnp.int32, (T, PAIR_W), 1)

    nrb = T // ROW_BLOCK
    chains = [(p, a, rb) for p in range(2) for a in range(2) for rb in range(nrb)]
    rs = [slice(rb * ROW_BLOCK, (rb + 1) * ROW_BLOCK) for rb in range(nrb)]

    def q_block(qi, _):
        q0 = pl.multiple_of(qi * T, T)
        qm = [q_ref[0, pl.ds(q0 + rb * ROW_BLOCK, ROW_BLOCK), p * PAIR_W:(p + 1) * PAIR_W] * hm_ref[p, a:a + 1, :]
              for p, a, rb in chains]

        def step(kj, carry, diag):
            k0 = pl.multiple_of(kj * T, T)
            kt = [k_ref[0, pl.ds(k0, T), p * PAIR_W:(p + 1) * PAIR_W] for p in range(2)]
            vt = [v_ref[0, pl.ds(k0, T), p * PAIR_W:(p + 1) * PAIR_W] for p in range(2)]
            kw = [(rb + 1) * ROW_BLOCK if diag else T for rb in range(nrb)]
            z = [_dot_nt(qm[c], kt[p][:kw[rb]]) for c, (p, a, rb) in enumerate(chains)]
            sp = []
            for (p, a, rb), zc in zip(chains, z):
                s_ = jnp.maximum(zc, 0.0) + jnp.log2(1.0 + jnp.exp2(-jnp.abs(zc)))
                if diag:
                    s_ = jnp.where(strict[rs[rb], :kw[rb]], s_, 0.0)
                sp.append(s_)
            cum = []
            for (p, a, rb), s_ in zip(chains, sp):
                sb = s_.astype(BF16)
                cum.append([_dot(sb[:, j * LANES:(j + 1) * LANES], uo_ref[...]) for j in range(kw[rb] // LANES)])
            w, runs = [], []
            for (p, a, rb), c, zc, s_, cm in zip(chains, carry, z, sp, cum):
                right = [None] * len(cm)
                run = c[1]
                for j in reversed(range(len(cm))):
                    right[j] = run
                    run = run + cm[j][:, LANES:]
                w_ = jnp.exp2(zc - s_ - jnp.concatenate([x[:, :LANES] for x in cm], axis=1)
                              - jnp.concatenate(right, axis=1))
                if diag:
                    w_ = jnp.where(strict[rs[rb], :kw[rb]], w_, 0.0)
                w.append(w_.astype(BF16))
                runs.append(run)
            accs = [c[0] + _dot(w_, vt[p][:kw[rb]]) for (p, a, rb), c, w_ in zip(chains, carry, w)]
            return tuple(zip(accs, runs))

        init = tuple((jnp.zeros((ROW_BLOCK, PAIR_W), F32), jnp.zeros((ROW_BLOCK, LANES), F32)) for _ in chains)
        carry = step(qi, init, True)
        carry = lax.fori_loop(0, qi, lambda i, c: step(qi - 1 - i, c, False), carry)
        for p in range(2):
            acc = [jnp.concatenate([carry[(2 * p + a) * nrb + rb][0] for rb in range(nrb)], axis=0) for a in range(2)]
            o = jnp.where(lane < HEAD_DIM, acc[0], acc[1])
            o_ref[0, pl.ds(q0, T), p * PAIR_W:(p + 1) * PAIR_W] = o.astype(o_ref.dtype)
        return 0

    lax.fori_loop(0, S // T, q_block, 0)


def _stick_attn(q, k, v, hm, uo):
    B, S, _ = v.shape
    spec = pl.BlockSpec((1, S, MIX_W), lambda b: (b, 0, 0))
    return pl.pallas_call(
        functools.partial(_stick_kernel, S),
        out_shape=jax.ShapeDtypeStruct((B, S, MIX_W), BF16),
        grid=(B,),
        in_specs=[spec, spec, spec, pl.BlockSpec(hm.shape, lambda b: (0, 0, 0)),
                  pl.BlockSpec(uo.shape, lambda b: (0, 0))],
        out_specs=spec,
        name="attn_stick",
        compiler_params=_cparams(("arbitrary",)),
    )(q, k, v, hm, uo)


def _route(scores_t, sel_t):
    G, E = N_GROUPS, EXPERTS_PER_GROUP
    sel = [sel_t[e:e + 1, :] for e in range(G * E)]
    sco = [scores_t[e:e + 1, :] for e in range(G * E)]
    gs = []
    for g in range(G):
        a, b, c, d = sel[E * g:E * g + E]
        gs.append(jnp.maximum(jnp.maximum(jnp.maximum(a + b, a + c), jnp.maximum(a + d, b + c)),
                              jnp.maximum(b + d, c + d)))
    gmax = jnp.maximum(jnp.maximum(gs[0], gs[1]), jnp.maximum(gs[2], gs[3]))
    grp = jnp.where(gs[0] == gmax, 0, jnp.where(gs[1] == gmax, 1, jnp.where(gs[2] == gmax, 2, 3)))

    def pick(rows, j):
        return jnp.where(grp == 0, rows[j], jnp.where(grp == 1, rows[E + j],
                                                      jnp.where(grp == 2, rows[2 * E + j], rows[3 * E + j])))

    ig = [pick(sel, j) for j in range(E)]
    igs = [pick(sco, j) for j in range(E)]

    def first_argmax(v):
        mx = jnp.maximum(jnp.maximum(v[0], v[1]), jnp.maximum(v[2], v[3]))
        return jnp.where(v[0] == mx, 0, jnp.where(v[1] == mx, 1, jnp.where(v[2] == mx, 2, 3)))

    l1 = first_argmax(ig)
    ig2 = [jnp.where(l1 == j, -jnp.inf, ig[j]) for j in range(E)]
    l2 = first_argmax(ig2)

    def take(v, idx):
        return jnp.where(idx == 0, v[0], jnp.where(idx == 1, v[1], jnp.where(idx == 2, v[2], v[3])))

    w1 = take(igs, l1)
    w2 = take(igs, l2)
    den = w1 + w2
    w1 = w1 / den
    w2 = w2 / den
    return grp, [jnp.where(l1 == j, w1, jnp.where(l2 == j, w2, 0.0)) for j in range(E)]


def _k5_kernel(x_ref, mod_ref, ng1_ref, ng2_ref, oa_ref, ob_ref, oc_ref, od_ref,
               wg_ref, wb_ref, wo_ref, wr_ref, rb_ref, x1_ref, h2_ref, cwp_ref, grp_ref):
    D = x_ref.shape[2]
    sh, sc, g_m = mod_ref[0, 0:1, :], mod_ref[0, 1:2, :], mod_ref[0, 2:3, :]
    sh_f, sc_f = mod_ref[0, 3:4, :], mod_ref[0, 4:5, :]
    tr = MERGE_ROW_BLOCK

    def merge(rs):
        hb = (_rms(x_ref[0, rs, :], ng1_ref[...]) * (1.0 + sc) + sh).astype(BF16)
        merged = None
        for b, o_ref in enumerate((oa_ref, ob_ref, oc_ref, od_ref)):
            gate = _sigmoid(_dot(hb, wg_ref[:, b * D:(b + 1) * D]))
            t = gate * _dot(o_ref[0, rs, :], wb_ref[b])
            merged = t if merged is None else merged + t
        return _dot(merged.astype(BF16), wo_ref[...])

    def ffn_norm(rs, y):
        x1 = x_ref[0, rs, :] + g_m * y
        x1_ref[0, rs, :] = x1
        h2 = (_rms(x1, ng2_ref[...]) * (1.0 + sc_f) + sh_f).astype(BF16)
        h2_ref[0, rs, :] = h2
        return _dot(h2, wr_ref[...])

    def route(rs, logits):
        st = _sigmoid(logits).T[0:N_EXPERTS, :]
        grp, cw = _route(st, st + rb_ref[...])
        grp_ref[0, :, rs] = grp
        nr = 16
        rid = lax.broadcasted_iota(jnp.int32, (nr, tr), 0)
        rec = jnp.where(rid == CW_GROUP_LANE, grp.astype(F32), 0.0)
        for j, c in enumerate(cw):
            hi = c.astype(BF16).astype(F32)
            mid = (c - hi).astype(BF16).astype(F32)
            for part, v in enumerate((hi, mid, c - hi - mid)):
                rec = jnp.where(rid == part * EXPERTS_PER_GROUP + j, v, rec)
        cwp_ref[0, rs, :] = jnp.concatenate([rec, jnp.zeros((LANES - nr, tr), F32)], axis=0).T.astype(BF16)

    blocks = [slice(r, r + tr) for r in range(0, x_ref.shape[1], tr)]
    y = merge(blocks[0])
    for i, rs in enumerate(blocks):
        y_next = merge(blocks[i + 1]) if i + 1 < len(blocks) else None
        route(rs, ffn_norm(rs, y))
        y = y_next


def _k5(x, mod, ng1, ng2, oa, ob, oc, od, lw, wr, rb, tm):
    B, S, D = x.shape
    bs = lambda w: pl.BlockSpec((1, tm, w), lambda b, i: (b, i, 0))
    full = lambda a: pl.BlockSpec(a.shape, lambda b, i: (0,) * a.ndim)
    consts = (lw["wg"], lw["wb"], lw["wo"], wr, rb)
    return pl.pallas_call(
        _k5_kernel,
        out_shape=(jax.ShapeDtypeStruct((B, S, D), F32), jax.ShapeDtypeStruct((B, S, D), BF16),
                   jax.ShapeDtypeStruct((B, S, LANES), BF16), jax.ShapeDtypeStruct((B, 1, S), jnp.int32)),
        grid=(B, S // tm),
        in_specs=[bs(D), pl.BlockSpec((1, 6, D), lambda b, i: (b, 0, 0)), full(ng1), full(ng2),
                  bs(MIX_W), bs(MIX_W), bs(MIX_W), bs(MIX_W)] + [full(a) for a in consts],
        out_specs=(bs(D), bs(D), bs(LANES), pl.BlockSpec((1, 1, tm), lambda b, i: (b, 0, i))),
        name="merge_route",
        compiler_params=_cparams(("arbitrary", "arbitrary")),
    )(x, mod, ng1, ng2, oa, ob, oc, od, *consts)


def _swiglu_rows(load_h, nrows, wgu, wd, weight_fn, store_y):
    chunks = [slice(r, r + MOE_ROW_BLOCK) for r in range(0, nrows, MOE_ROW_BLOCK)]

    def up(rs):
        h = load_h(rs)
        return [_dot(h, w) for w in wgu]

    nxt = up(chunks[0])
    for i, rs in enumerate(chunks):
        gu = nxt
        if i + 1 < len(chunks):
            nxt = up(chunks[i + 1])
        acts = []
        for j, gu_j in enumerate(gu):
            g = gu_j[:, :D_EXPERT]
            a = g * _sigmoid(g) * gu_j[:, D_EXPERT:]
            w = weight_fn(rs, j)
            acts.append((a if w is None else a * w).astype(BF16))
        store_y(rs, _dot(acts[0] if len(acts) == 1 else jnp.concatenate(acts, axis=1), wd))


def _moe_plan(grp, n_tokens):
    ntd = n_tokens // MOE_TOKEN_TILE
    g = grp.reshape(ntd, MOE_TOKEN_TILE)
    cnt = jnp.sum((g[:, :, None] == jnp.arange(N_GROUPS, dtype=jnp.int32)).astype(jnp.int32), axis=1)
    c_al = (cnt + MOE_ALIGN - 1) // MOE_ALIGN * MOE_ALIGN
    c_ch = (cnt + MOE_CHUNK - 1) // MOE_CHUNK * MOE_CHUNK
    lo = jnp.cumsum(c_ch, axis=1) - c_ch
    tot = jnp.sum(c_al, axis=0)
    reg = (tot + MOE_CHUNK + MOE_EXPERT_TILE - 1) // MOE_EXPERT_TILE * MOE_EXPERT_TILE
    end = jnp.cumsum(reg)
    goff = (end - reg)[None, :] + jnp.cumsum(c_al, axis=0) - c_al
    start = jnp.arange(_moe_buffer_rows(n_tokens) // MOE_EXPERT_TILE, dtype=jnp.int32) * MOE_EXPERT_TILE
    tile_group = jnp.minimum(jnp.sum((start[:, None] >= end[None, :]).astype(jnp.int32), axis=1), N_GROUPS - 1)
    tile_valid = (start < end[-1]).astype(jnp.int32)
    flat = lambda a: a.reshape(-1).astype(jnp.int32)
    return flat(goff), flat(lo), flat(c_ch // MOE_CHUNK), tile_group.astype(jnp.int32), tile_valid


def _moe_buffer_rows(n_tokens):
    ntd = n_tokens // MOE_TOKEN_TILE
    worst = n_tokens + (MOE_ALIGN - 1) * N_GROUPS * ntd + N_GROUPS * (MOE_CHUNK + MOE_EXPERT_TILE)
    return (worst + MOE_EXPERT_TILE - 1) // MOE_EXPERT_TILE * MOE_EXPERT_TILE


def _segment_copies(i, goff_ref, lo_ref, nch_ref, make_copy, action):
    for g in range(N_GROUPS):
        lo_g = lo_ref[i * N_GROUPS + g]
        go_g = goff_ref[i * N_GROUPS + g]

        def body(c, _):
            action(make_copy(pl.multiple_of(lo_g + c * MOE_CHUNK, MOE_ALIGN),
                             pl.multiple_of(go_g + c * MOE_CHUNK, MOE_ALIGN)))
            return 0

        lax.fori_loop(0, nch_ref[i * N_GROUPS + g], body, 0)


def _dispatch_kernel(goff_ref, lo_ref, nch_ref, h2_ref, cwp_ref, grp_ref, ust_ref, xs_in_ref, xs_ref, xl_ref, sem):
    del xs_in_ref
    i = pl.program_id(0)
    T, R = MOE_TOKEN_TILE, MOE_LOCAL_ROWS
    gid = lax.broadcasted_iota(jnp.int32, (8, T), 0)
    onehot = jnp.where(grp_ref[0] == gid, 1.0, 0.0)
    rank = _dot(onehot.astype(BF16), ust_ref[...])
    lo_rows = jnp.zeros((8, T), F32)
    for g in range(N_GROUPS):
        lo_rows = jnp.where(gid == g, lo_ref[i * N_GROUPS + g].astype(F32), lo_rows)
    lp = jnp.sum(onehot * (rank + lo_rows), axis=0, keepdims=True).astype(jnp.int32)
    perm = jnp.where(lp == lax.broadcasted_iota(jnp.int32, (R, T), 0), 1.0, 0.0).astype(BF16)
    slot = i % 2
    xl_ref[slot] = _dot(perm, jnp.concatenate([h2_ref[...], cwp_ref[...]], axis=1)).astype(BF16)

    def make_copy(s):
        return lambda lo, go: pltpu.make_async_copy(
            xl_ref.at[s, pl.ds(lo, MOE_CHUNK)], xs_ref.at[pl.ds(go, MOE_CHUNK)], sem.at[s])

    @pl.when(i > 0)
    def _():
        _segment_copies(i - 1, goff_ref, lo_ref, nch_ref, make_copy(1 - slot), lambda cp: cp.wait())

    _segment_copies(i, goff_ref, lo_ref, nch_ref, make_copy(slot), lambda cp: cp.start())

    @pl.when(i == pl.num_programs(0) - 1)
    def _():
        _segment_copies(i, goff_ref, lo_ref, nch_ref, make_copy(slot), lambda cp: cp.wait())


def _dispatch(plan, h2, cwp, grp, ust):
    N, D = h2.shape
    goff, lo, nch, _, _ = plan
    T = MOE_TOKEN_TILE
    rows = _moe_buffer_rows(N)
    xs0 = jnp.zeros((rows, D + LANES), BF16)
    grid_spec = pltpu.PrefetchScalarGridSpec(
        num_scalar_prefetch=3,
        grid=(N // T,),
        in_specs=[pl.BlockSpec((T, D), lambda i, *_: (i, 0)),
                  pl.BlockSpec((T, LANES), lambda i, *_: (i, 0)),
                  pl.BlockSpec((1, 1, T), lambda i, *_: (i, 0, 0)),
                  pl.BlockSpec((T, T), lambda i, *_: (0, 0)),
                  pl.BlockSpec(memory_space=pl.ANY)],
        out_specs=pl.BlockSpec(memory_space=pl.ANY),
        scratch_shapes=[pltpu.VMEM((2, MOE_LOCAL_ROWS, D + LANES), BF16), pltpu.SemaphoreType.DMA((2,))],
    )
    return pl.pallas_call(
        _dispatch_kernel,
        out_shape=jax.ShapeDtypeStruct((rows, D + LANES), BF16),
        grid_spec=grid_spec,
        input_output_aliases={7: 0},
        name="moe_dispatch",
        compiler_params=_cparams(("arbitrary",)),
    )(goff, lo, nch, h2, cwp, grp.reshape(N // T, 1, T), ust, xs0)


def _expert_kernel(tg_ref, tv_ref, xs_ref, wgu_ref, wd_ref, ys_ref):
    del tg_ref
    i = pl.program_id(0)
    D = ys_ref.shape[1]
    E = EXPERTS_PER_GROUP
    lane = lax.broadcasted_iota(jnp.int32, (MOE_ROW_BLOCK, LANES), 1)

    @pl.when(tv_ref[i] != 0)
    def _():
        def weight(rs, j):
            rec = xs_ref[rs, D:].astype(F32)
            return jnp.sum(jnp.where((lane % E == j) & (lane < 3 * E), rec, 0.0), axis=-1, keepdims=True)

        def store(rs, y):
            ys_ref[rs, :] = y.astype(ys_ref.dtype)

        _swiglu_rows(lambda rs: xs_ref[rs, :D], MOE_EXPERT_TILE, [wgu_ref[0, j] for j in range(E)],
                     wd_ref[0], weight, store)

    @pl.when(tv_ref[i] == 0)
    def _():
        ys_ref[...] = jnp.zeros_like(ys_ref)


def _experts(plan, xs, wgu, wd):
    rows, XW = xs.shape
    D = XW - LANES
    _, _, _, tile_group, tile_valid = plan
    E = EXPERTS_PER_GROUP
    TE = MOE_EXPERT_TILE
    grid_spec = pltpu.PrefetchScalarGridSpec(
        num_scalar_prefetch=2,
        grid=(rows // TE,),
        in_specs=[pl.BlockSpec((TE, XW), lambda i, tg, tv: (i, 0)),
                  pl.BlockSpec((1, E, D, 2 * D_EXPERT), lambda i, tg, tv: (tg[i], 0, 0, 0)),
                  pl.BlockSpec((1, E * D_EXPERT, D), lambda i, tg, tv: (tg[i], 0, 0))],
        out_specs=pl.BlockSpec((TE, D), lambda i, tg, tv: (i, 0)),
    )
    return pl.pallas_call(
        _expert_kernel,
        out_shape=jax.ShapeDtypeStruct((rows, D), BF16),
        grid_spec=grid_spec,
        name="moe_experts",
        compiler_params=_cparams(("arbitrary",)),
    )(tile_group, tile_valid, xs, wgu, wd)


def _combine_kernel(goff_ref, lo_ref, nch_ref, ys_ref, x1_ref, h2_ref, cwp_ref, mod_ref, lst_ref,
                    wgu_s_ref, wd_s_ref, out_ref, yl_ref, sem):
    i = pl.program_id(0)
    n = pl.num_programs(0)
    T, R = MOE_TOKEN_TILE, MOE_LOCAL_ROWS

    def make_copy(slot):
        return lambda lo, go: pltpu.make_async_copy(
            ys_ref.at[pl.ds(go, MOE_CHUNK)], yl_ref.at[slot, pl.ds(lo, MOE_CHUNK)], sem.at[slot])

    def fetch(t, slot):
        _segment_copies(t, goff_ref, lo_ref, nch_ref, make_copy(slot), lambda cp: cp.start())

    slot = i % 2

    @pl.when(i == 0)
    def _():
        fetch(0, 0)

    @pl.when(i + 1 < n)
    def _():
        fetch(i + 1, 1 - slot)

    lane = lax.broadcasted_iota(jnp.int32, (T, LANES), 1)
    gcol = cwp_ref[:, CW_GROUP_LANE:CW_GROUP_LANE + 1].astype(F32).astype(jnp.int32)
    onehot = jnp.where(gcol == lane, 1.0, 0.0)
    rank = _dot(lst_ref[...], onehot.astype(BF16))
    lo_lanes = jnp.zeros((T, LANES), F32)
    for g in range(N_GROUPS):
        lo_lanes = jnp.where(lane == g, lo_ref[i * N_GROUPS + g].astype(F32), lo_lanes)
    lp = jnp.sum(onehot * (rank + lo_lanes), axis=-1, keepdims=True).astype(jnp.int32)
    perm_t = jnp.where(lp == lax.broadcasted_iota(jnp.int32, (T, R), 1), 1.0, 0.0).astype(BF16)

    acc = []
    _swiglu_rows(lambda rs: h2_ref[rs, :], T, [wgu_s_ref[...]], wd_s_ref[...], lambda rs, j: None,
                 lambda rs, y: acc.append(y))
    shared = acc[0] if len(acc) == 1 else jnp.concatenate(acc, axis=0)

    _segment_copies(i, goff_ref, lo_ref, nch_ref, make_copy(slot), lambda cp: cp.wait())
    used = lo_ref[i * N_GROUPS + N_GROUPS - 1] + nch_ref[i * N_GROUPS + N_GROUPS - 1] * MOE_CHUNK
    row = lax.broadcasted_iota(jnp.int32, (R, out_ref.shape[1]), 0)
    yl = jnp.where(row < used, yl_ref[slot], jnp.zeros((), yl_ref.dtype))
    out_ref[...] = x1_ref[...] + mod_ref[0, 5:6, :] * (shared + _dot(perm_t, yl))


def _combine(plan, ys, x1, h2, cwp, mod, lst, wgu_s, wd_s, S):
    N, D = h2.shape
    goff, lo, nch, _, _ = plan
    T = MOE_TOKEN_TILE
    per_b = S // T
    row = lambda w: pl.BlockSpec((T, w), lambda i, *_: (i, 0))
    grid_spec = pltpu.PrefetchScalarGridSpec(
        num_scalar_prefetch=3,
        grid=(N // T,),
        in_specs=[pl.BlockSpec(memory_space=pl.ANY), row(D), row(D), row(LANES),
                  pl.BlockSpec((1, 6, D), lambda i, *_: (i // per_b, 0, 0)),
                  pl.BlockSpec((T, T), lambda i, *_: (0, 0)),
                  pl.BlockSpec((D, 2 * D_EXPERT), lambda i, *_: (0, 0)),
                  pl.BlockSpec((D_EXPERT, D), lambda i, *_: (0, 0))],
        out_specs=row(D),
        scratch_shapes=[pltpu.VMEM((2, MOE_LOCAL_ROWS, D), BF16), pltpu.SemaphoreType.DMA((2,))],
    )
    return pl.pallas_call(
        _combine_kernel,
        out_shape=jax.ShapeDtypeStruct((N, D), F32),
        grid_spec=grid_spec,
        name="moe_combine",
        compiler_params=_cparams(("arbitrary",)),
    )(goff, lo, nch, ys, x1, h2, cwp, mod, lst, wgu_s, wd_s)


def _const_tables():
    lane = np.arange(MIX_W)
    bd = (lane[:, None] // HEAD_DIM == lane[None, :] // HEAD_DIM).astype(np.float32)
    mp = np.zeros((2, MIX_W, MIX_W), np.float32)
    hm_c = np.zeros((2, 2, MIX_W), np.float32)
    for p in range(2):
        head = np.where(lane < LANES, 2 * p + lane // HEAD_DIM, ((lane - LANES) % HEAD_DIM) // C_HALF)
        mp[p] = head[:, None] == head[None, :]
        for a in range(2):
            hm_c[p, a] = head == 2 * p + a
    hm = np.zeros((2, 2, PAIR_W), np.float32)
    for a in range(2):
        hm[:, a] = (np.arange(PAIR_W) // HEAD_DIM == a)
    j = np.arange(LANES)
    ustrict = (j[:, None] > j[None, :]).astype(np.float32)
    uo = np.concatenate([ustrict, np.ones((LANES, LANES), np.float32)], axis=1)
    t = np.arange(MOE_TOKEN_TILE)
    earlier = (t[:, None] < t[None, :]).astype(np.float32)
    return (jnp.asarray(bd, BF16), jnp.asarray(mp, BF16), jnp.asarray(hm, BF16),
            jnp.asarray(hm_c, BF16), jnp.asarray(uo, BF16), jnp.asarray(earlier, BF16),
            jnp.asarray(earlier.T, BF16))


def _band_bias(rel_bias):
    L, H, _ = rel_bias.shape
    R, Wn = BAND_TQ, BAND_WIN
    ncase = BAND_LEFT // BAND_TQ + 1
    off = (np.arange(ncase) * BAND_TQ)[:, None]
    n = R + Wn
    k = np.arange(n)[None, :]
    dist = np.where(k <= Wn, off - k, off + n - k)
    idx = np.clip(dist, -D_MAX_REL, D_MAX_REL) + D_MAX_REL
    u = rel_bias[:, :, jnp.asarray(idx)]
    bias = jnp.tile(u, (1, 1, 1, R))[..., :R * (n - 1)].reshape(L, H, ncase, R, n - 1)[..., :Wn]
    r = np.arange(R)[None, :, None]
    c = np.arange(Wn)[None, None, :]
    kc = np.floor_divide(c - off[:, :, None], CHUNK)
    rc = r // CHUNK
    vis = (kc <= rc) & (kc >= rc - D_LEFT_CHUNKS)
    return jnp.where(jnp.asarray(vis)[None, None], bias * LOG2E, NEG_INF).astype(F32)


def _pack_weights(w_in, fox_forget_b, fox_q_g, fox_k_g, mla_cq_g, mla_ckv_g, mla_w_uq, mla_w_ukv,
                  mla_q_g, mla_k_g, chunk_q_g, chunk_k_g, w_branch, w_out):
    L, D, _ = w_in.shape
    W = MIX_W
    a0, b0 = 0, 3 * W + N_HEADS
    c0 = b0 + 3 * W
    d0 = c0 + C_Q_RANK + C_KV_RANK + C_ROPE
    g0 = d0 + 3 * W
    scale = HEAD_DIM ** -0.5
    kr = w_in[:, :, c0 + C_Q_RANK + C_KV_RANK:d0]
    w1 = jnp.concatenate([
        w_in[:, :, a0:a0 + 3 * W],
        w_in[:, :, b0:b0 + W] * (scale * LOG2E), w_in[:, :, b0 + W:b0 + 3 * W],
        w_in[:, :, d0:d0 + 3 * W],
        w_in[:, :, c0:c0 + C_Q_RANK + C_KV_RANK],
        jnp.tile(kr[:, :, :C_HALF], (1, 1, N_HEADS)), jnp.tile(kr[:, :, C_HALF:], (1, 1, N_HEADS)),
    ], axis=2).astype(BF16)
    wfa = jnp.pad(jnp.swapaxes(w_in[:, :, 3 * W:3 * W + N_HEADS], 1, 2), ((0, 0), (0, 8 - N_HEADS), (0, 0))).astype(BF16)
    fb = jnp.pad(fox_forget_b, ((0, 0), (0, 8 - N_HEADS))).reshape(L, 8, 1).astype(F32)
    wg = w_in[:, :, g0:].astype(BF16)

    uq = mla_w_uq.reshape(L, C_Q_RANK, N_HEADS, C_QK)
    uq_rope = jnp.concatenate([uq[..., C_NOPE:C_NOPE + C_HALF].reshape(L, C_Q_RANK, N_HEADS * C_HALF),
                               uq[..., C_NOPE + C_HALF:].reshape(L, C_Q_RANK, N_HEADS * C_HALF)], axis=2)
    wuq = jnp.concatenate([uq[:, :, 0, :C_NOPE], uq[:, :, 1, :C_NOPE], uq_rope,
                           uq[:, :, 2, :C_NOPE], uq[:, :, 3, :C_NOPE], uq_rope], axis=2).astype(BF16)
    ukv = mla_w_ukv.reshape(L, C_KV_RANK, N_HEADS, C_NOPE + HEAD_DIM)
    wukv = jnp.concatenate([ukv[..., :C_NOPE].reshape(L, C_KV_RANK, W),
                            ukv[..., C_NOPE:].reshape(L, C_KV_RANK, W)], axis=2).astype(BF16)

    def c_gain(g, s):
        nope = g[:, :C_NOPE]
        rope = jnp.concatenate([jnp.tile(g[:, C_NOPE:C_NOPE + C_HALF], (1, N_HEADS)),
                                jnp.tile(g[:, C_NOPE + C_HALF:], (1, N_HEADS))], axis=1)
        pair = jnp.concatenate([nope, nope, rope], axis=1)
        return jnp.concatenate([pair, pair], axis=1) * s

    def pad512(v):
        return jnp.pad(v, ((0, 0), (0, 2 * W - v.shape[1])))

    gv = jnp.stack([
        pad512(jnp.tile(fox_q_g, (1, N_HEADS)) * (scale * LOG2E)), pad512(jnp.tile(fox_k_g, (1, N_HEADS))),
        pad512(jnp.tile(chunk_q_g, (1, N_HEADS)) * (scale * LOG2E)), pad512(jnp.tile(chunk_k_g, (1, N_HEADS))),
        pad512(mla_cq_g), pad512(mla_ckv_g), c_gain(mla_q_g, C_QK ** -0.5 * LOG2E), c_gain(mla_k_g, 1.0),
    ], axis=1).astype(F32)
    return dict(w1=w1, wfa=wfa, fb=fb, wg=wg, wuq=wuq, wukv=wukv, gv=gv,
                wb=w_branch.astype(BF16), wo=w_out.astype(BF16))


def kernel(x, c, positions, norm_mix_g, norm_ffn_g, w_ada, b_ada, w_in, fox_forget_b, fox_q_g, fox_k_g, mla_cq_g, mla_ckv_g, mla_w_uq, mla_w_ukv, mla_q_g, mla_k_g, chunk_q_g, chunk_k_g, chunk_rel_bias, w_branch, w_out, router_w, router_b, exp_w_gate, exp_w_up, exp_w_down, sh_w_gate, sh_w_up, sh_w_down):
    B, S, D = x.shape
    L = w_in.shape[0]
    assert S % ATT_BLOCK == 0 and S >= BAND_WIN and D % LANES == 0
    tm_p1 = min(S, 512)
    tm_k5 = min(S, 512)

    bd, mp, hm, hm_c, uo, ust, lst = _const_tables()
    pw = _pack_weights(w_in, fox_forget_b, fox_q_g, fox_k_g, mla_cq_g, mla_ckv_g, mla_w_uq, mla_w_ukv,
                       mla_q_g, mla_k_g, chunk_q_g, chunk_k_g, w_branch, w_out)
    band_bias = _band_bias(chunk_rel_bias)
    wr = jnp.pad(router_w, ((0, 0), (0, LANES - N_EXPERTS))).astype(BF16)
    rb = router_b.reshape(N_EXPERTS, 1).astype(F32)
    wgu = jnp.concatenate([exp_w_gate, exp_w_up], axis=-1).astype(BF16).reshape(
        L, N_GROUPS, EXPERTS_PER_GROUP, D, 2 * D_EXPERT)
    wd = exp_w_down.astype(BF16).reshape(L, N_GROUPS, EXPERTS_PER_GROUP * D_EXPERT, D)
    wgu_s = jnp.concatenate([sh_w_gate, sh_w_up], axis=-1).astype(BF16)
    wd_s = sh_w_down.astype(BF16)

    mod = _modulation(c, w_ada, b_ada).reshape(L, B, 6, D)
    cosl, sinl = _rope_tables(positions)

    for l in range(L):
        lw = {k: v[l] for k, v in pw.items()}
        lw["bd"], lw["mp"] = bd, mp
        ng1 = norm_mix_g[l].reshape(1, D)
        ng2 = norm_ffn_g[l].reshape(1, D)
        (qa, ka, va, lf, qb, kb, vb, qc, kc, vc, qd, kd, vd) = _p1(x, mod[l], ng1, lw, cosl, sinl, tm_p1)
        oa = _softmax_attn("fox", qa, ka, va, hm, lf)
        ob = _stick_attn(qb, kb, vb, hm, uo)
        oc = _softmax_attn("mla", qc, kc, vc, hm_c)
        od = _band_attn(qd, kd, vd, hm, band_bias[l])
        x1, h2, cwp, grp = _k5(x, mod[l], ng1, ng2, oa, ob, oc, od, lw, wr, rb, tm_k5)
        h2, cwp = h2.reshape(B * S, D), cwp.reshape(B * S, LANES)
        plan = _moe_plan(grp, B * S)
        xs = _dispatch(plan, h2, cwp, grp, ust)
        ys = _experts(plan, xs, wgu[l], wd[l])
        x = _combine(plan, ys, x1.reshape(B * S, D), h2, cwp, mod[l], lst, wgu_s[l], wd_s[l], S).reshape(B, S, D)
    return x
```

```python
import functools
import math

import numpy as np
import jax
import jax.numpy as jnp
from jax import lax
from jax.experimental import pallas as pl
from jax.experimental.pallas import tpu as pltpu

F32 = jnp.float32
BF16 = jnp.bfloat16

HEAD_DIM = 64
N_HEADS = 4
MIX_W = N_HEADS * HEAD_DIM
CHUNK = 64
NEG_INF = -1e30
EPS = 1e-6
C_Q_RANK = 256
C_KV_RANK = 128
C_NOPE = 64
C_ROPE = 32
C_HALF = C_ROPE // 2
C_QK = C_NOPE + C_ROPE
ROPE_BASE = 10000.0
D_LEFT_CHUNKS = 8
D_MAX_REL = 128
N_BRANCH = 4
N_EXPERTS = 16
N_GROUPS = 4
EXPERTS_PER_GROUP = 4
D_EXPERT = 256
LOG2E = math.log2(math.e)

LANES = 128
ATT_BLOCK = 512
ROW_BLOCK = 256
SOFTMAX_ROW_BLOCK = 512
MERGE_ROW_BLOCK = 256
MOE_ROW_BLOCK = 256
MOE_TOKEN_TILE = 256
MOE_ALIGN = 16
MOE_CHUNK = 32
MOE_LOCAL_ROWS = MOE_TOKEN_TILE + N_GROUPS * MOE_CHUNK
MOE_EXPERT_TILE = 512
CW_GROUP_LANE = 3 * EXPERTS_PER_GROUP
BAND_TQ = 256
BAND_LEFT = D_LEFT_CHUNKS * CHUNK
BAND_WIN = BAND_LEFT + BAND_TQ
PAIR_W = 2 * HEAD_DIM
VMEM_LIMIT = 56 * 1024 * 1024

P1_COLS = 9 * MIX_W + C_Q_RANK + C_KV_RANK + LANES


def _cparams(sem):
    return pltpu.CompilerParams(dimension_semantics=sem, vmem_limit_bytes=VMEM_LIMIT)


def _dot(a, b):
    return jnp.dot(a, b, preferred_element_type=F32)


def _dot_nt(a, b):
    return lax.dot_general(a, b, (((1,), (1,)), ((), ())), preferred_element_type=F32)


def _sigmoid(x):
    return 1.0 / (1.0 + jnp.exp(-x))


def _rms(x, g):
    return x * lax.rsqrt(jnp.mean(x * x, axis=-1, keepdims=True) + EPS) * g


def _mod_kernel(c_ref, w_ref, b_ref, o_ref):
    c = c_ref[...]
    cond = c * _sigmoid(c)
    o_ref[0] = jnp.dot(cond, w_ref[0], preferred_element_type=F32,
                       precision=lax.Precision.HIGHEST) + b_ref[0]


def _modulation(c, w_ada, b_ada):
    L, D, D6 = w_ada.shape
    B = c.shape[0]
    tn = 1024
    return pl.pallas_call(
        _mod_kernel,
        out_shape=jax.ShapeDtypeStruct((L, B, D6), F32),
        grid=(L, D6 // tn),
        in_specs=[pl.BlockSpec((B, D), lambda l, j: (0, 0)),
                  pl.BlockSpec((1, D, tn), lambda l, j: (l, 0, j)),
                  pl.BlockSpec((1, 1, tn), lambda l, j: (l, 0, j))],
        out_specs=pl.BlockSpec((1, B, tn), lambda l, j: (l, 0, j)),
        name="adaln_mod",
        compiler_params=_cparams(("arbitrary", "arbitrary")),
    )(c, w_ada, b_ada.reshape(L, 1, D6))


def _rope_kernel(ang_ref, cos_ref, sin_ref):
    a = ang_ref[0]
    cos_ref[0] = jnp.cos(a)
    sin_ref[0] = jnp.sin(a)


def _rope_tables(positions):
    B, S = positions.shape
    inv_freq = jnp.power(ROPE_BASE, -jnp.arange(C_HALF, dtype=F32) / C_HALF)
    ang = positions.astype(F32)[:, :, None] * inv_freq[None, None, :]
    rep = LANES // C_HALF
    ang = ang.reshape(B, S // rep, LANES)
    spec = pl.BlockSpec((1, S // rep, LANES), lambda b: (b, 0, 0))
    cos, sin = pl.pallas_call(
        _rope_kernel,
        out_shape=(jax.ShapeDtypeStruct(ang.shape, F32),) * 2,
        grid=(B,),
        in_specs=[spec],
        out_specs=(spec, spec),
        name="rope_tables",
        compiler_params=_cparams(("arbitrary",)),
    )(ang)
    sign = jnp.where(jnp.arange(LANES) < LANES // 2, -1.0, 1.0).astype(F32)
    cosl = jnp.tile(cos.reshape(B, S, C_HALF), (1, 1, rep))
    sinl = jnp.tile(sin.reshape(B, S, C_HALF), (1, 1, rep)) * sign
    return cosl, sinl


def _rope(t, cos, sin):
    return t * cos + pltpu.roll(t, LANES // 2, 1) * sin


def _p1_kernel(x_ref, mod_ref, ng_ref, w1_ref, wfa_ref, fb_ref, gv_ref, wuq_ref, wukv_ref,
               bd_ref, mp_ref, cos_ref, sin_ref,
               qa_ref, ka_ref, va_ref, lf_ref, qb_ref, kb_ref, vb_ref,
               qc_ref, kc_ref, vc_ref, qd_ref, kd_ref, vd_ref):
    x = x_ref[0]
    sh = mod_ref[0, 0:1, :]
    sc = mod_ref[0, 1:2, :]
    hb = (_rms(x, ng_ref[...]) * (1.0 + sc) + sh).astype(BF16)
    W = MIX_W

    def mm(c0, c1):
        return _dot(hb, w1_ref[:, c0:c1])

    def head_norm(t, g):
        ss = _dot((t * t).astype(BF16), bd_ref[...])
        return (t * lax.rsqrt(ss * (1.0 / HEAD_DIM) + EPS) * g).astype(BF16)

    r_a = mm(0, 3 * W)
    fa = _dot_nt(wfa_ref[...], hb) + fb_ref[...]
    r_b = mm(3 * W, 6 * W)

    qa_ref[0] = head_norm(r_a[:, 0:W], gv_ref[0:1, 0:W])
    ka_ref[0] = head_norm(r_a[:, W:2 * W], gv_ref[1:2, 0:W])
    va_ref[0] = r_a[:, 2 * W:3 * W].astype(BF16)
    lf_ref[0] = jnp.minimum(fa, 0.0) - jnp.log(1.0 + jnp.exp(-jnp.abs(fa)))
    r_d = mm(6 * W, 9 * W)

    qb_ref[0] = r_b[:, 0:W].astype(BF16)
    kb_ref[0] = r_b[:, W:2 * W].astype(BF16)
    vb_ref[0] = r_b[:, 2 * W:3 * W].astype(BF16)
    c0 = 9 * W
    r = mm(c0, c0 + C_Q_RANK + C_KV_RANK + LANES)

    qd_ref[0] = head_norm(r_d[:, 0:W], gv_ref[2:3, 0:W])
    kd_ref[0] = head_norm(r_d[:, W:2 * W], gv_ref[3:4, 0:W])
    vd_ref[0] = r_d[:, 2 * W:3 * W].astype(BF16)

    cos = cos_ref[0]
    sin = sin_ref[0]
    cq = _rms(r[:, 0:C_Q_RANK], gv_ref[4:5, 0:C_Q_RANK]).astype(BF16)
    ckv = _rms(r[:, C_Q_RANK:C_Q_RANK + C_KV_RANK], gv_ref[5:6, 0:C_KV_RANK]).astype(BF16)
    kr = _rope(r[:, C_Q_RANK + C_KV_RANK:], cos, sin)
    qu = _dot(cq, wuq_ref[...])
    kvu = _dot(ckv, wukv_ref[...])
    vc_ref[0] = kvu[:, 2 * LANES:].astype(BF16)
    for p in range(2):
        qp = jnp.concatenate([qu[:, p * 2 * LANES:p * 2 * LANES + LANES],
                              _rope(qu[:, p * 2 * LANES + LANES:(p + 1) * 2 * LANES], cos, sin)], axis=1)
        kp = jnp.concatenate([kvu[:, p * LANES:(p + 1) * LANES], kr], axis=1)
        for t, gr, dst in ((qp, 6, qc_ref), (kp, 7, kc_ref)):
            ss = _dot((t * t).astype(BF16), mp_ref[p])
            tn = t * lax.rsqrt(ss * (1.0 / C_QK) + EPS) * gv_ref[gr:gr + 1, p * 2 * LANES:(p + 1) * 2 * LANES]
            dst[0, :, p * 2 * LANES:(p + 1) * 2 * LANES] = tn.astype(BF16)


def _p1(x, mod, ng, lw, cosl, sinl, tm):
    B, S, D = x.shape
    W = MIX_W
    bs = lambda w: pl.BlockSpec((1, tm, w), lambda b, i: (b, i, 0))
    full = lambda a: pl.BlockSpec(a.shape, lambda b, i: (0,) * a.ndim)
    o256 = jax.ShapeDtypeStruct((B, S, W), BF16)
    o512 = jax.ShapeDtypeStruct((B, S, 2 * W), BF16)
    out_shape = (o256, o256, o256, jax.ShapeDtypeStruct((B, 8, S), F32),
                 o256, o256, o256, o512, o512, o256, o256, o256, o256)
    out_specs = (bs(W), bs(W), bs(W), pl.BlockSpec((1, 8, tm), lambda b, i: (b, 0, i)),
                 bs(W), bs(W), bs(W), bs(2 * W), bs(2 * W), bs(W), bs(W), bs(W), bs(W))
    consts = (ng, lw["w1"], lw["wfa"], lw["fb"], lw["gv"], lw["wuq"], lw["wukv"], lw["bd"], lw["mp"])
    return pl.pallas_call(
        _p1_kernel,
        out_shape=out_shape,
        grid=(B, S // tm),
        in_specs=[bs(D), pl.BlockSpec((1, 6, D), lambda b, i: (b, 0, 0))]
                 + [full(a) for a in consts] + [bs(LANES), bs(LANES)],
        out_specs=out_specs,
        name="in_proj",
        compiler_params=_cparams(("arbitrary", "arbitrary")),
    )(x, mod, *consts, cosl, sinl)


def _lane_cumsum(x):
    n = x.shape[1]
    idx = lax.broadcasted_iota(jnp.int32, x.shape, 1)
    k = 1
    while k < n:
        x = x + jnp.where(idx >= k, pltpu.roll(x, k, 1), 0.0)
        k *= 2
    return x


def _ones_outside(own, v):
    return jnp.where(own[:v.shape[0]], v, jnp.ones_like(v))


def _normalize_pair(acc0, acc1, lane):
    half = PAIR_W // 2
    o0 = acc0 / pltpu.roll(acc0, half, 1)
    o1 = acc1 / pltpu.roll(acc1, half, 1)
    return jnp.where(lane < HEAD_DIM, o0, o1)


def _softmax_attn_kernel(mode, S, *refs):
    T = ATT_BLOCK
    if mode == "fox":
        q_ref, k_ref, v_ref, hm_ref, lf_ref, o_ref, cf_ref = refs
        cf = _lane_cumsum(lf_ref[0]) * LOG2E
        for h in range(N_HEADS):
            cf_ref[h] = cf[h:h + 1, :]
    else:
        q_ref, k_ref, v_ref, hm_ref, o_ref = refs
    cw = q_ref.shape[2] // 2
    row = lax.broadcasted_iota(jnp.int32, (T, T), 0)
    col = lax.broadcasted_iota(jnp.int32, (T, T), 1)
    if mode == "fox":
        dmask = col <= row
    else:
        dmask = (col // CHUNK) <= (row // CHUNK)
    lane = lax.broadcasted_iota(jnp.int32, (T, PAIR_W), 1)
    own = [lane // HEAD_DIM == a for a in range(2)]
    heads = [(p, a) for p in range(2) for a in range(2)]

    def q_block(qi, _):
        q0 = pl.multiple_of(qi * T, T)
        qm = [q_ref[0, pl.ds(q0, T), p * cw:(p + 1) * cw] * hm_ref[p, a:a + 1, :] for p, a in heads]

        def step(kj, carry, diag):
            k0 = pl.multiple_of(kj * T, T)
            kt = [k_ref[0, pl.ds(k0, T), p * cw:(p + 1) * cw] for p in range(2)]
            vt = [v_ref[0, pl.ds(k0, T), p * PAIR_W:(p + 1) * PAIR_W] for p in range(2)]
            s = [_dot_nt(qm[h], kt[p]) for h, (p, a) in enumerate(heads)]
            if mode == "fox":
                s = [s[h] - cf_ref[h, :, pl.ds(k0, T)] for h in range(N_HEADS)]
            if diag:
                s = [jnp.where(dmask, sc, NEG_INF) for sc in s]
            m_new = [jnp.maximum(c[0], jnp.max(sc, axis=-1, keepdims=True)) for c, sc in zip(carry, s)]
            pe = [jnp.exp2(sc - mn).astype(BF16) for sc, mn in zip(s, m_new)]
            acc = [jnp.exp2(c[0] - mn) * c[1] + _dot(p_, _ones_outside(own[a], vt[p]))
                   for (p, a), c, mn, p_ in zip(heads, carry, m_new, pe)]
            return tuple(zip(m_new, acc))

        init = tuple((jnp.full((T, 1), NEG_INF, F32), jnp.zeros((T, PAIR_W), F32)) for _ in range(N_HEADS))
        carry = lax.fori_loop(0, qi, lambda kj, c: step(kj, c, False), init)
        carry = step(qi, carry, True)
        for p in range(2):
            o = _normalize_pair(carry[2 * p][1], carry[2 * p + 1][1], lane)
            o_ref[0, pl.ds(q0, T), p * PAIR_W:(p + 1) * PAIR_W] = o.astype(o_ref.dtype)
        return 0

    lax.fori_loop(0, S // T, q_block, 0)


def _softmax_attn(mode, q, k, v, hm, lf=None):
    B, S, _ = v.shape
    spec = lambda w: pl.BlockSpec((1, S, w), lambda b: (b, 0, 0))
    in_specs = [spec(q.shape[2]), spec(q.shape[2]), spec(MIX_W), pl.BlockSpec(hm.shape, lambda b: (0, 0, 0))]
    args = [q, k, v, hm]
    scratch = []
    if mode == "fox":
        in_specs.append(pl.BlockSpec((1, 8, S), lambda b: (b, 0, 0)))
        args.append(lf)
        scratch = [pltpu.VMEM((N_HEADS, 1, S), F32)]
    return pl.pallas_call(
        functools.partial(_softmax_attn_kernel, mode, S),
        out_shape=jax.ShapeDtypeStruct((B, S, MIX_W), BF16),
        grid=(B,),
        in_specs=in_specs,
        out_specs=spec(MIX_W),
        scratch_shapes=scratch,
        name="attn_" + mode,
        compiler_params=_cparams(("arbitrary",)),
    )(*args)


def _band_kernel(S, q_ref, k_ref, v_ref, hm_ref, bias_ref, o_ref):
    lane = lax.broadcasted_iota(jnp.int32, (BAND_TQ, PAIR_W), 1)
    vlane = lax.broadcasted_iota(jnp.int32, (BAND_WIN, PAIR_W), 1)
    own = [vlane // HEAD_DIM == a for a in range(2)]

    def q_block(qi, _):
        q0 = pl.multiple_of(qi * BAND_TQ, BAND_TQ)
        k0 = pl.multiple_of(jnp.maximum(q0 - BAND_LEFT, 0), BAND_TQ)
        case = jnp.minimum(qi, BAND_LEFT // BAND_TQ)
        heads = [(p, a) for p in range(2) for a in range(2)]
        ps = [slice(p * PAIR_W, (p + 1) * PAIR_W) for p in range(2)]
        kt = [k_ref[0, pl.ds(k0, BAND_WIN), ps[p]] for p in range(2)]
        vt = [v_ref[0, pl.ds(k0, BAND_WIN), ps[p]] for p in range(2)]
        s = [_dot_nt(q_ref[0, pl.ds(q0, BAND_TQ), ps[p]] * hm_ref[p, a:a + 1, :], kt[p])
             + bias_ref[2 * p + a, case] for p, a in heads]
        pe = [jnp.exp2(sc - jnp.max(sc, axis=-1, keepdims=True)).astype(BF16) for sc in s]
        accs = [_dot(p_, _ones_outside(own[a], vt[p])) for (p, a), p_ in zip(heads, pe)]
        for p in range(2):
            o = _normalize_pair(accs[2 * p], accs[2 * p + 1], lane)
            o_ref[0, pl.ds(q0, BAND_TQ), ps[p]] = o.astype(o_ref.dtype)
        return 0

    lax.fori_loop(0, S // BAND_TQ, q_block, 0)


def _band_attn(q, k, v, hm, bias):
    B, S, _ = v.shape
    spec = pl.BlockSpec((1, S, MIX_W), lambda b: (b, 0, 0))
    return pl.pallas_call(
        functools.partial(_band_kernel, S),
        out_shape=jax.ShapeDtypeStruct((B, S, MIX_W), BF16),
        grid=(B,),
        in_specs=[spec, spec, spec, pl.BlockSpec(hm.shape, lambda b: (0, 0, 0)),
                  pl.BlockSpec(bias.shape, lambda b: (0, 0, 0, 0))],
        out_specs=spec,
        name="attn_band",
        compiler_params=_cparams(("arbitrary",)),
    )(q, k, v, hm, bias)


def _stick_kernel(S, q_ref, k_ref, v_ref, hm_ref, uo_ref, o_ref):
    T = ATT_BLOCK
    row = lax.broadcasted_iota(jnp.int32, (T, T), 0)
    col = lax.broadcasted_iota(jnp.int32, (T, T), 1)
    strict = col < row
    lane = lax.broadcasted_iota(jnp.int32, (T, PAIR_W), 1)

    nrb = T // ROW_BLOCK
    chains = [(p, a, rb) for p in range(2) for a in range(2) for rb in range(nrb)]
    rs = [slice(rb * ROW_BLOCK, (rb + 1) * ROW_BLOCK) for rb in range(nrb)]

    def q_block(qi, _):
        q0 = pl.multiple_of(qi * T, T)
        qm = [q_ref[0, pl.ds(q0 + rb * ROW_BLOCK, ROW_BLOCK), p * PAIR_W:(p + 1) * PAIR_W] * hm_ref[p, a:a + 1, :]
              for p, a, rb in chains]

        def step(kj, carry, diag):
            k0 = pl.multiple_of(kj * T, T)
            kt = [k_ref[0, pl.ds(k0, T), p * PAIR_W:(p + 1) * PAIR_W] for p in range(2)]
            vt = [v_ref[0, pl.ds(k0, T), p * PAIR_W:(p + 1) * PAIR_W] for p in range(2)]
            kw = [(rb + 1) * ROW_BLOCK if diag else T for rb in range(nrb)]
            z = [_dot_nt(qm[c], kt[p][:kw[rb]]) for c, (p, a, rb) in enumerate(chains)]
            sp = []
            for (p, a, rb), zc in zip(chains, z):
                s_ = jnp.maximum(zc, 0.0) + jnp.log2(1.0 + jnp.exp2(-jnp.abs(zc)))
                if diag:
                    s_ = jnp.where(strict[rs[rb], :kw[rb]], s_, 0.0)
                sp.append(s_)
            cum = []
            for (p, a, rb), s_ in zip(chains, sp):
                sb = s_.astype(BF16)
                cum.append([_dot(sb[:, j * LANES:(j + 1) * LANES], uo_ref[...]) for j in range(kw[rb] // LANES)])
            w, runs = [], []
            for (p, a, rb), c, zc, s_, cm in zip(chains, carry, z, sp, cum):
                right = [None] * len(cm)
                run = c[1]
                for j in reversed(range(len(cm))):
                    right[j] = run
                    run = run + cm[j][:, LANES:]
                w_ = jnp.exp2(zc - s_ - jnp.concatenate([x[:, :LANES] for x in cm], axis=1)
                              - jnp.concatenate(right, axis=1))
                if diag:
                    w_ = jnp.where(strict[rs[rb], :kw[rb]], w_, 0.0)
                w.append(w_.astype(BF16))
                runs.append(run)
            accs = [c[0] + _dot(w_, vt[p][:kw[rb]]) for (p, a, rb), c, w_ in zip(chains, carry, w)]
            return tuple(zip(accs, runs))

        init = tuple((jnp.zeros((ROW_BLOCK, PAIR_W), F32), jnp.zeros((ROW_BLOCK, LANES), F32)) for _ in chains)
        carry = step(qi, init, True)
        carry = lax.fori_loop(0, qi, lambda i, c: step(qi - 1 - i, c, False), carry)
        for p in range(2):
            acc = [jnp.concatenate([carry[(2 * p + a) * nrb + rb][0] for rb in range(nrb)], axis=0) for a in range(2)]
            o = jnp.where(lane < HEAD_DIM, acc[0], acc[1])
            o_ref[0, pl.ds(q0, T), p * PAIR_W:(p + 1) * PAIR_W] = o.astype(o_ref.dtype)
        return 0

    lax.fori_loop(0, S // T, q_block, 0)


def _stick_attn(q, k, v, hm, uo):
    B, S, _ = v.shape
    spec = pl.BlockSpec((1, S, MIX_W), lambda b: (b, 0, 0))
    return pl.pallas_call(
        functools.partial(_stick_kernel, S),
        out_shape=jax.ShapeDtypeStruct((B, S, MIX_W), BF16),
        grid=(B,),
        in_specs=[spec, spec, spec, pl.BlockSpec(hm.shape, lambda b: (0, 0, 0)),
                  pl.BlockSpec(uo.shape, lambda b: (0, 0))],
        out_specs=spec,
        name="attn_stick",
        compiler_params=_cparams(("arbitrary",)),
    )(q, k, v, hm, uo)


def _route(scores_t, sel_t):
    G, E = N_GROUPS, EXPERTS_PER_GROUP
    sel = [sel_t[e:e + 1, :] for e in range(G * E)]
    sco = [scores_t[e:e + 1, :] for e in range(G * E)]
    gs = []
    for g in range(G):
        a, b, c, d = sel[E * g:E * g + E]
        gs.append(jnp.maximum(jnp.maximum(jnp.maximum(a + b, a + c), jnp.maximum(a + d, b + c)),
                              jnp.maximum(b + d, c + d)))
    gmax = jnp.maximum(jnp.maximum(gs[0], gs[1]), jnp.maximum(gs[2], gs[3]))
    grp = jnp.where(gs[0] == gmax, 0, jnp.where(gs[1] == gmax, 1, jnp.where(gs[2] == gmax, 2, 3)))

    def pick(rows, j):
        return jnp.where(grp == 0, rows[j], jnp.where(grp == 1, rows[E + j],
                                                      jnp.where(grp == 2, rows[2 * E + j], rows[3 * E + j])))

    ig = [pick(sel, j) for j in range(E)]
    igs = [pick(sco, j) for j in range(E)]

    def first_argmax(v):
        mx = jnp.maximum(jnp.maximum(v[0], v[1]), jnp.maximum(v[2], v[3]))
        return jnp.where(v[0] == mx, 0, jnp.where(v[1] == mx, 1, jnp.where(v[2] == mx, 2, 3)))

    l1 = first_argmax(ig)
    ig2 = [jnp.where(l1 == j, -jnp.inf, ig[j]) for j in range(E)]
    l2 = first_argmax(ig2)

    def take(v, idx):
        return jnp.where(idx == 0, v[0], jnp.where(idx == 1, v[1], jnp.where(idx == 2, v[2], v[3])))

    w1 = take(igs, l1)
    w2 = take(igs, l2)
    den = w1 + w2
    w1 = w1 / den
    w2 = w2 / den
    return grp, [jnp.where(l1 == j, w1, jnp.where(l2 == j, w2, 0.0)) for j in range(E)]


def _k5_kernel(x_ref, mod_ref, ng1_ref, ng2_ref, oa_ref, ob_ref, oc_ref, od_ref,
               wg_ref, wb_ref, wo_ref, wr_ref, rb_ref, x1_ref, h2_ref, cwp_ref, grp_ref):
    D = x_ref.shape[2]
    sh, sc, g_m = mod_ref[0, 0:1, :], mod_ref[0, 1:2, :], mod_ref[0, 2:3, :]
    sh_f, sc_f = mod_ref[0, 3:4, :], mod_ref[0, 4:5, :]
    tr = MERGE_ROW_BLOCK

    def merge(rs):
        hb = (_rms(x_ref[0, rs, :], ng1_ref[...]) * (1.0 + sc) + sh).astype(BF16)
        merged = None
        for b, o_ref in enumerate((oa_ref, ob_ref, oc_ref, od_ref)):
            gate = _sigmoid(_dot(hb, wg_ref[:, b * D:(b + 1) * D]))
            t = gate * _dot(o_ref[0, rs, :], wb_ref[b])
            merged = t if merged is None else merged + t
        return _dot(merged.astype(BF16), wo_ref[...])

    def ffn_norm(rs, y):
        x1 = x_ref[0, rs, :] + g_m * y
        x1_ref[0, rs, :] = x1
        h2 = (_rms(x1, ng2_ref[...]) * (1.0 + sc_f) + sh_f).astype(BF16)
        h2_ref[0, rs, :] = h2
        return _dot(h2, wr_ref[...])

    def route(rs, logits):
        st = _sigmoid(logits).T[0:N_EXPERTS, :]
        grp, cw = _route(st, st + rb_ref[...])
        grp_ref[0, :, rs] = grp
        nr = 16
        rid = lax.broadcasted_iota(jnp.int32, (nr, tr), 0)
        rec = jnp.where(rid == CW_GROUP_LANE, grp.astype(F32), 0.0)
        for j, c in enumerate(cw):
            hi = c.astype(BF16).astype(F32)
            mid = (c - hi).astype(BF16).astype(F32)
            for part, v in enumerate((hi, mid, c - hi - mid)):
                rec = jnp.where(rid == part * EXPERTS_PER_GROUP + j, v, rec)
        cwp_ref[0, rs, :] = jnp.concatenate([rec, jnp.zeros((LANES - nr, tr), F32)], axis=0).T.astype(BF16)

    blocks = [slice(r, r + tr) for r in range(0, x_ref.shape[1], tr)]
    y = merge(blocks[0])
    for i, rs in enumerate(blocks):
        y_next = merge(blocks[i + 1]) if i + 1 < len(blocks) else None
        route(rs, ffn_norm(rs, y))
        y = y_next


def _k5(x, mod, ng1, ng2, oa, ob, oc, od, lw, wr, rb, tm):
    B, S, D = x.shape
    bs = lambda w: pl.BlockSpec((1, tm, w), lambda b, i: (b, i, 0))
    full = lambda a: pl.BlockSpec(a.shape, lambda b, i: (0,) * a.ndim)
    consts = (lw["wg"], lw["wb"], lw["wo"], wr, rb)
    return pl.pallas_call(
        _k5_kernel,
        out_shape=(jax.ShapeDtypeStruct((B, S, D), F32), jax.ShapeDtypeStruct((B, S, D), BF16),
                   jax.ShapeDtypeStruct((B, S, LANES), BF16), jax.ShapeDtypeStruct((B, 1, S), jnp.int32)),
        grid=(B, S // tm),
        in_specs=[bs(D), pl.BlockSpec((1, 6, D), lambda b, i: (b, 0, 0)), full(ng1), full(ng2),
                  bs(MIX_W), bs(MIX_W), bs(MIX_W), bs(MIX_W)] + [full(a) for a in consts],
        out_specs=(bs(D), bs(D), bs(LANES), pl.BlockSpec((1, 1, tm), lambda b, i: (b, 0, i))),
        name="merge_route",
        compiler_params=_cparams(("arbitrary", "arbitrary")),
    )(x, mod, ng1, ng2, oa, ob, oc, od, *consts)


def _swiglu_rows(load_h, nrows, wgu, wd, weight_fn, store_y):
    chunks = [slice(r, r + MOE_ROW_BLOCK) for r in range(0, nrows, MOE_ROW_BLOCK)]

    def up(rs):
        h = load_h(rs)
        return [_dot(h, w) for w in wgu]

    nxt = up(chunks[0])
    for i, rs in enumerate(chunks):
        gu = nxt
        if i + 1 < len(chunks):
            nxt = up(chunks[i + 1])
        acts = []
        for j, gu_j in enumerate(gu):
            g = gu_j[:, :D_EXPERT]
            a = g * _sigmoid(g) * gu_j[:, D_EXPERT:]
            w = weight_fn(rs, j)
            acts.append((a if w is None else a * w).astype(BF16))
        store_y(rs, _dot(acts[0] if len(acts) == 1 else jnp.concatenate(acts, axis=1), wd))


def _moe_plan(grp, n_tokens):
    ntd = n_tokens // MOE_TOKEN_TILE
    g = grp.reshape(ntd, MOE_TOKEN_TILE)
    cnt = jnp.sum((g[:, :, None] == jnp.arange(N_GROUPS, dtype=jnp.int32)).astype(jnp.int32), axis=1)
    c_al = (cnt + MOE_ALIGN - 1) // MOE_ALIGN * MOE_ALIGN
    c_ch = (cnt + MOE_CHUNK - 1) // MOE_CHUNK * MOE_CHUNK
    lo = jnp.cumsum(c_ch, axis=1) - c_ch
    tot = jnp.sum(c_al, axis=0)
    reg = (tot + MOE_CHUNK + MOE_EXPERT_TILE - 1) // MOE_EXPERT_TILE * MOE_EXPERT_TILE
    end = jnp.cumsum(reg)
    goff = (end - reg)[None, :] + jnp.cumsum(c_al, axis=0) - c_al
    start = jnp.arange(_moe_buffer_rows(n_tokens) // MOE_EXPERT_TILE, dtype=jnp.int32) * MOE_EXPERT_TILE
    tile_group = jnp.minimum(jnp.sum((start[:, None] >= end[None, :]).astype(jnp.int32), axis=1), N_GROUPS - 1)
    tile_valid = (start < end[-1]).astype(jnp.int32)
    flat = lambda a: a.reshape(-1).astype(jnp.int32)
    return flat(goff), flat(lo), flat(c_ch // MOE_CHUNK), tile_group.astype(jnp.int32), tile_valid


def _moe_buffer_rows(n_tokens):
    ntd = n_tokens // MOE_TOKEN_TILE
    worst = n_tokens + (MOE_ALIGN - 1) * N_GROUPS * ntd + N_GROUPS * (MOE_CHUNK + MOE_EXPERT_TILE)
    return (worst + MOE_EXPERT_TILE - 1) // MOE_EXPERT_TILE * MOE_EXPERT_TILE


def _segment_copies(i, goff_ref, lo_ref, nch_ref, make_copy, action):
    for g in range(N_GROUPS):
        lo_g = lo_ref[i * N_GROUPS + g]
        go_g = goff_ref[i * N_GROUPS + g]

        def body(c, _):
            action(make_copy(pl.multiple_of(lo_g + c * MOE_CHUNK, MOE_ALIGN),
                             pl.multiple_of(go_g + c * MOE_CHUNK, MOE_ALIGN)))
            return 0

        lax.fori_loop(0, nch_ref[i * N_GROUPS + g], body, 0)


def _dispatch_kernel(goff_ref, lo_ref, nch_ref, h2_ref, cwp_ref, grp_ref, ust_ref, xs_in_ref, xs_ref, xl_ref, sem):
    del xs_in_ref
    i = pl.program_id(0)
    T, R = MOE_TOKEN_TILE, MOE_LOCAL_ROWS
    gid = lax.broadcasted_iota(jnp.int32, (8, T), 0)
    onehot = jnp.where(grp_ref[0] == gid, 1.0, 0.0)
    rank = _dot(onehot.astype(BF16), ust_ref[...])
    lo_rows = jnp.zeros((8, T), F32)
    for g in range(N_GROUPS):
        lo_rows = jnp.where(gid == g, lo_ref[i * N_GROUPS + g].astype(F32), lo_rows)
    lp = jnp.sum(onehot * (rank + lo_rows), axis=0, keepdims=True).astype(jnp.int32)
    perm = jnp.where(lp == lax.broadcasted_iota(jnp.int32, (R, T), 0), 1.0, 0.0).astype(BF16)
    slot = i % 2
    xl_ref[slot] = _dot(perm, jnp.concatenate([h2_ref[...], cwp_ref[...]], axis=1)).astype(BF16)

    def make_copy(s):
        return lambda lo, go: pltpu.make_async_copy(
            xl_ref.at[s, pl.ds(lo, MOE_CHUNK)], xs_ref.at[pl.ds(go, MOE_CHUNK)], sem.at[s])

    @pl.when(i > 0)
    def _():
        _segment_copies(i - 1, goff_ref, lo_ref, nch_ref, make_copy(1 - slot), lambda cp: cp.wait())

    _segment_copies(i, goff_ref, lo_ref, nch_ref, make_copy(slot), lambda cp: cp.start())

    @pl.when(i == pl.num_programs(0) - 1)
    def _():
        _segment_copies(i, goff_ref, lo_ref, nch_ref, make_copy(slot), lambda cp: cp.wait())


def _dispatch(plan, h2, cwp, grp, ust, xs0):
    N, D = h2.shape
    goff, lo, nch, _, _ = plan
    T = MOE_TOKEN_TILE
    rows = _moe_buffer_rows(N)
    assert xs0.shape == (rows, D + LANES) and xs0.dtype == BF16
    grid_spec = pltpu.PrefetchScalarGridSpec(
        num_scalar_prefetch=3,
        grid=(N // T,),
        in_specs=[pl.BlockSpec((T, D), lambda i, *_: (i, 0)),
                  pl.BlockSpec((T, LANES), lambda i, *_: (i, 0)),
                  pl.BlockSpec((1, 1, T), lambda i, *_: (i, 0, 0)),
                  pl.BlockSpec((T, T), lambda i, *_: (0, 0)),
                  pl.BlockSpec(memory_space=pl.ANY)],
        out_specs=pl.BlockSpec(memory_space=pl.ANY),
        scratch_shapes=[pltpu.VMEM((2, MOE_LOCAL_ROWS, D + LANES), BF16), pltpu.SemaphoreType.DMA((2,))],
    )
    return pl.pallas_call(
        _dispatch_kernel,
        out_shape=jax.ShapeDtypeStruct((rows, D + LANES), BF16),
        grid_spec=grid_spec,
        input_output_aliases={7: 0},
        name="moe_dispatch",
        compiler_params=_cparams(("arbitrary",)),
    )(goff, lo, nch, h2, cwp, grp.reshape(N // T, 1, T), ust, xs0)


def _expert_kernel(tg_ref, tv_ref, xs_ref, wgu_ref, wd_ref, ys_ref):
    del tg_ref
    i = pl.program_id(0)
    D = ys_ref.shape[1]
    E = EXPERTS_PER_GROUP
    lane = lax.broadcasted_iota(jnp.int32, (MOE_ROW_BLOCK, LANES), 1)

    @pl.when(tv_ref[i] != 0)
    def _():
        def weight(rs, j):
            rec = xs_ref[rs, D:].astype(F32)
            return jnp.sum(jnp.where((lane % E == j) & (lane < 3 * E), rec, 0.0), axis=-1, keepdims=True)

        def store(rs, y):
            ys_ref[rs, :] = y.astype(ys_ref.dtype)

        _swiglu_rows(lambda rs: xs_ref[rs, :D], MOE_EXPERT_TILE, [wgu_ref[0, j] for j in range(E)],
                     wd_ref[0], weight, store)

    @pl.when(tv_ref[i] == 0)
    def _():
        ys_ref[...] = jnp.zeros_like(ys_ref)


def _experts(plan, xs, wgu, wd):
    rows, XW = xs.shape
    D = XW - LANES
    _, _, _, tile_group, tile_valid = plan
    E = EXPERTS_PER_GROUP
    TE = MOE_EXPERT_TILE
    grid_spec = pltpu.PrefetchScalarGridSpec(
        num_scalar_prefetch=2,
        grid=(rows // TE,),
        in_specs=[pl.BlockSpec((TE, XW), lambda i, tg, tv: (i, 0)),
                  pl.BlockSpec((1, E, D, 2 * D_EXPERT), lambda i, tg, tv: (tg[i], 0, 0, 0)),
                  pl.BlockSpec((1, E * D_EXPERT, D), lambda i, tg, tv: (tg[i], 0, 0))],
        out_specs=pl.BlockSpec((TE, D), lambda i, tg, tv: (i, 0)),
    )
    return pl.pallas_call(
        _expert_kernel,
        out_shape=jax.ShapeDtypeStruct((rows, D), BF16),
        grid_spec=grid_spec,
        name="moe_experts",
        compiler_params=_cparams(("arbitrary",)),
    )(tile_group, tile_valid, xs, wgu, wd)


def _combine_kernel(goff_ref, lo_ref, nch_ref, ys_ref, x1_ref, h2_ref, cwp_ref, mod_ref, lst_ref,
                    wgu_s_ref, wd_s_ref, out_ref, yl_ref, sem):
    i = pl.program_id(0)
    n = pl.num_programs(0)
    T, R = MOE_TOKEN_TILE, MOE_LOCAL_ROWS

    def make_copy(slot):
        return lambda lo, go: pltpu.make_async_copy(
            ys_ref.at[pl.ds(go, MOE_CHUNK)], yl_ref.at[slot, pl.ds(lo, MOE_CHUNK)], sem.at[slot])

    def fetch(t, slot):
        _segment_copies(t, goff_ref, lo_ref, nch_ref, make_copy(slot), lambda cp: cp.start())

    slot = i % 2

    @pl.when(i == 0)
    def _():
        fetch(0, 0)

    @pl.when(i + 1 < n)
    def _():
        fetch(i + 1, 1 - slot)

    lane = lax.broadcasted_iota(jnp.int32, (T, LANES), 1)
    gcol = cwp_ref[:, CW_GROUP_LANE:CW_GROUP_LANE + 1].astype(F32).astype(jnp.int32)
    onehot = jnp.where(gcol == lane, 1.0, 0.0)
    rank = _dot(lst_ref[...], onehot.astype(BF16))
    lo_lanes = jnp.zeros((T, LANES), F32)
    for g in range(N_GROUPS):
        lo_lanes = jnp.where(lane == g, lo_ref[i * N_GROUPS + g].astype(F32), lo_lanes)
    lp = jnp.sum(onehot * (rank + lo_lanes), axis=-1, keepdims=True).astype(jnp.int32)
    perm_t = jnp.where(lp == lax.broadcasted_iota(jnp.int32, (T, R), 1), 1.0, 0.0).astype(BF16)

    acc = []
    _swiglu_rows(lambda rs: h2_ref[rs, :], T, [wgu_s_ref[...]], wd_s_ref[...], lambda rs, j: None,
                 lambda rs, y: acc.append(y))
    shared = acc[0] if len(acc) == 1 else jnp.concatenate(acc, axis=0)

    _segment_copies(i, goff_ref, lo_ref, nch_ref, make_copy(slot), lambda cp: cp.wait())
    used = lo_ref[i * N_GROUPS + N_GROUPS - 1] + nch_ref[i * N_GROUPS + N_GROUPS - 1] * MOE_CHUNK
    row = lax.broadcasted_iota(jnp.int32, (R, out_ref.shape[1]), 0)
    yl = jnp.where(row < used, yl_ref[slot], jnp.zeros((), yl_ref.dtype))
    out_ref[...] = x1_ref[...] + mod_ref[0, 5:6, :] * (shared + _dot(perm_t, yl))


def _combine(plan, ys, x1, h2, cwp, mod, lst, wgu_s, wd_s, S):
    N, D = h2.shape
    goff, lo, nch, _, _ = plan
    T = MOE_TOKEN_TILE
    per_b = S // T
    row = lambda w: pl.BlockSpec((T, w), lambda i, *_: (i, 0))
    grid_spec = pltpu.PrefetchScalarGridSpec(
        num_scalar_prefetch=3,
        grid=(N // T,),
        in_specs=[pl.BlockSpec(memory_space=pl.ANY), row(D), row(D), row(LANES),
                  pl.BlockSpec((1, 6, D), lambda i, *_: (i // per_b, 0, 0)),
                  pl.BlockSpec((T, T), lambda i, *_: (0, 0)),
                  pl.BlockSpec((D, 2 * D_EXPERT), lambda i, *_: (0, 0)),
                  pl.BlockSpec((D_EXPERT, D), lambda i, *_: (0, 0))],
        out_specs=row(D),
        scratch_shapes=[pltpu.VMEM((2, MOE_LOCAL_ROWS, D), BF16), pltpu.SemaphoreType.DMA((2,))],
    )
    return pl.pallas_call(
        _combine_kernel,
        out_shape=jax.ShapeDtypeStruct((N, D), F32),
        grid_spec=grid_spec,
        name="moe_combine",
        compiler_params=_cparams(("arbitrary",)),
    )(goff, lo, nch, ys, x1, h2, cwp, mod, lst, wgu_s, wd_s)


def _const_tables():
    lane = np.arange(MIX_W)
    bd = (lane[:, None] // HEAD_DIM == lane[None, :] // HEAD_DIM).astype(np.float32)
    mp = np.zeros((2, MIX_W, MIX_W), np.float32)
    hm_c = np.zeros((2, 2, MIX_W), np.float32)
    for p in range(2):
        head = np.where(lane < LANES, 2 * p + lane // HEAD_DIM, ((lane - LANES) % HEAD_DIM) // C_HALF)
        mp[p] = head[:, None] == head[None, :]
        for a in range(2):
            hm_c[p, a] = head == 2 * p + a
    hm = np.zeros((2, 2, PAIR_W), np.float32)
    for a in range(2):
        hm[:, a] = (np.arange(PAIR_W) // HEAD_DIM == a)
    j = np.arange(LANES)
    ustrict = (j[:, None] > j[None, :]).astype(np.float32)
    uo = np.concatenate([ustrict, np.ones((LANES, LANES), np.float32)], axis=1)
    t = np.arange(MOE_TOKEN_TILE)
    earlier = (t[:, None] < t[None, :]).astype(np.float32)
    return (jnp.asarray(bd, BF16), jnp.asarray(mp, BF16), jnp.asarray(hm, BF16),
            jnp.asarray(hm_c, BF16), jnp.asarray(uo, BF16), jnp.asarray(earlier, BF16),
            jnp.asarray(earlier.T, BF16))


def _band_bias(rel_bias):
    L, H, _ = rel_bias.shape
    R, Wn = BAND_TQ, BAND_WIN
    ncase = BAND_LEFT // BAND_TQ + 1
    off = (np.arange(ncase) * BAND_TQ)[:, None]
    n = R + Wn
    k = np.arange(n)[None, :]
    dist = np.where(k <= Wn, off - k, off + n - k)
    idx = np.clip(dist, -D_MAX_REL, D_MAX_REL) + D_MAX_REL
    u = rel_bias[:, :, jnp.asarray(idx)]
    bias = jnp.tile(u, (1, 1, 1, R))[..., :R * (n - 1)].reshape(L, H, ncase, R, n - 1)[..., :Wn]
    r = np.arange(R)[None, :, None]
    c = np.arange(Wn)[None, None, :]
    kc = np.floor_divide(c - off[:, :, None], CHUNK)
    rc = r // CHUNK
    vis = (kc <= rc) & (kc >= rc - D_LEFT_CHUNKS)
    return jnp.where(jnp.asarray(vis)[None, None], bias * LOG2E, NEG_INF).astype(F32)


def _pack_weights(w_in, fox_forget_b, fox_q_g, fox_k_g, mla_cq_g, mla_ckv_g, mla_w_uq, mla_w_ukv,
                  mla_q_g, mla_k_g, chunk_q_g, chunk_k_g, w_branch, w_out):
    L, D, _ = w_in.shape
    W = MIX_W
    a0, b0 = 0, 3 * W + N_HEADS
    c0 = b0 + 3 * W
    d0 = c0 + C_Q_RANK + C_KV_RANK + C_ROPE
    g0 = d0 + 3 * W
    scale = HEAD_DIM ** -0.5
    kr = w_in[:, :, c0 + C_Q_RANK + C_KV_RANK:d0]
    w1 = jnp.concatenate([
        w_in[:, :, a0:a0 + 3 * W],
        w_in[:, :, b0:b0 + W] * (scale * LOG2E), w_in[:, :, b0 + W:b0 + 3 * W],
        w_in[:, :, d0:d0 + 3 * W],
        w_in[:, :, c0:c0 + C_Q_RANK + C_KV_RANK],
        jnp.tile(kr[:, :, :C_HALF], (1, 1, N_HEADS)), jnp.tile(kr[:, :, C_HALF:], (1, 1, N_HEADS)),
    ], axis=2).astype(BF16)
    wfa = jnp.pad(jnp.swapaxes(w_in[:, :, 3 * W:3 * W + N_HEADS], 1, 2), ((0, 0), (0, 8 - N_HEADS), (0, 0))).astype(BF16)
    fb = jnp.pad(fox_forget_b, ((0, 0), (0, 8 - N_HEADS))).reshape(L, 8, 1).astype(F32)
    wg = w_in[:, :, g0:].astype(BF16)

    uq = mla_w_uq.reshape(L, C_Q_RANK, N_HEADS, C_QK)
    uq_rope = jnp.concatenate([uq[..., C_NOPE:C_NOPE + C_HALF].reshape(L, C_Q_RANK, N_HEADS * C_HALF),
                               uq[..., C_NOPE + C_HALF:].reshape(L, C_Q_RANK, N_HEADS * C_HALF)], axis=2)
    wuq = jnp.concatenate([uq[:, :, 0, :C_NOPE], uq[:, :, 1, :C_NOPE], uq_rope,
                           uq[:, :, 2, :C_NOPE], uq[:, :, 3, :C_NOPE], uq_rope], axis=2).astype(BF16)
    ukv = mla_w_ukv.reshape(L, C_KV_RANK, N_HEADS, C_NOPE + HEAD_DIM)
    wukv = jnp.concatenate([ukv[..., :C_NOPE].reshape(L, C_KV_RANK, W),
                            ukv[..., C_NOPE:].reshape(L, C_KV_RANK, W)], axis=2).astype(BF16)

    def c_gain(g, s):
        nope = g[:, :C_NOPE]
        rope = jnp.concatenate([jnp.tile(g[:, C_NOPE:C_NOPE + C_HALF], (1, N_HEADS)),
                                jnp.tile(g[:, C_NOPE + C_HALF:], (1, N_HEADS))], axis=1)
        pair = jnp.concatenate([nope, nope, rope], axis=1)
        return jnp.concatenate([pair, pair], axis=1) * s

    def pad512(v):
        return jnp.pad(v, ((0, 0), (0, 2 * W - v.shape[1])))

    gv = jnp.stack([
        pad512(jnp.tile(fox_q_g, (1, N_HEADS)) * (scale * LOG2E)), pad512(jnp.tile(fox_k_g, (1, N_HEADS))),
        pad512(jnp.tile(chunk_q_g, (1, N_HEADS)) * (scale * LOG2E)), pad512(jnp.tile(chunk_k_g, (1, N_HEADS))),
        pad512(mla_cq_g), pad512(mla_ckv_g), c_gain(mla_q_g, C_QK ** -0.5 * LOG2E), c_gain(mla_k_g, 1.0),
    ], axis=1).astype(F32)
    return dict(w1=w1, wfa=wfa, fb=fb, wg=wg, wuq=wuq, wukv=wukv, gv=gv,
                wb=w_branch.astype(BF16), wo=w_out.astype(BF16))


def kernel(x, c, positions, norm_mix_g, norm_ffn_g, w_ada, b_ada, w_in, fox_forget_b, fox_q_g, fox_k_g, mla_cq_g, mla_ckv_g, mla_w_uq, mla_w_ukv, mla_q_g, mla_k_g, chunk_q_g, chunk_k_g, chunk_rel_bias, w_branch, w_out, router_w, router_b, exp_w_gate, exp_w_up, exp_w_down, sh_w_gate, sh_w_up, sh_w_down):
    B, S, D = x.shape
    L = w_in.shape[0]
    assert S % ATT_BLOCK == 0 and S >= BAND_WIN and D % LANES == 0
    tm_p1 = min(S, 512)
    tm_k5 = min(S, 512)

    bd, mp, hm, hm_c, uo, ust, lst = _const_tables()
    pw = _pack_weights(w_in, fox_forget_b, fox_q_g, fox_k_g, mla_cq_g, mla_ckv_g, mla_w_uq, mla_w_ukv,
                       mla_q_g, mla_k_g, chunk_q_g, chunk_k_g, w_branch, w_out)
    band_bias = _band_bias(chunk_rel_bias)
    wr = jnp.pad(router_w, ((0, 0), (0, LANES - N_EXPERTS))).astype(BF16)
    rb = router_b.reshape(N_EXPERTS, 1).astype(F32)
    wgu = jnp.concatenate([exp_w_gate, exp_w_up], axis=-1).astype(BF16).reshape(
        L, N_GROUPS, EXPERTS_PER_GROUP, D, 2 * D_EXPERT)
    wd = exp_w_down.astype(BF16).reshape(L, N_GROUPS, EXPERTS_PER_GROUP * D_EXPERT, D)
    wgu_s = jnp.concatenate([sh_w_gate, sh_w_up], axis=-1).astype(BF16)
    wd_s = sh_w_down.astype(BF16)

    mod = _modulation(c, w_ada, b_ada).reshape(L, B, 6, D)
    cosl, sinl = _rope_tables(positions)

    xs = jnp.zeros((_moe_buffer_rows(B * S), D + LANES), BF16)
    for l in range(L):
        lw = {k: v[l] for k, v in pw.items()}
        lw["bd"], lw["mp"] = bd, mp
        ng1 = norm_mix_g[l].reshape(1, D)
        ng2 = norm_ffn_g[l].reshape(1, D)
        (qa, ka, va, lf, qb, kb, vb, qc, kc, vc, qd, kd, vd) = _p1(x, mod[l], ng1, lw, cosl, sinl, tm_p1)
        oa = _softmax_attn("fox", qa, ka, va, hm, lf)
        ob = _stick_attn(qb, kb, vb, hm, uo)
        oc = _softmax_attn("mla", qc, kc, vc, hm_c)
        od = _band_attn(qd, kd, vd, hm, band_bias[l])
        x1, h2, cwp, grp = _k5(x, mod[l], ng1, ng2, oa, ob, oc, od, lw, wr, rb, tm_k5)
        h2, cwp = h2.reshape(B * S, D), cwp.reshape(B * S, LANES)
        plan = _moe_plan(grp, B * S)
        xs = _dispatch(plan, h2, cwp, grp, ust, xs)
        ys = _experts(plan, xs, wgu[l], wd[l])
        x = _combine(plan, ys, x1.reshape(B * S, D), h2, cwp, mod[l], lst, wgu_s[l], wd_s[l], S).reshape(B, S, D)
    return x
```

```python
import functools
import math

import numpy as np
import jax
import jax.numpy as jnp
from jax import lax
from jax.experimental import pallas as pl
from jax.experimental.pallas import tpu as pltpu

F32 = jnp.float32
BF16 = jnp.bfloat16

HEAD_DIM = 64
N_HEADS = 4
MIX_W = N_HEADS * HEAD_DIM
CHUNK = 64
NEG_INF = -1e30
EPS = 1e-6
C_Q_RANK = 256
C_KV_RANK = 128
C_NOPE = 64
C_ROPE = 32
C_HALF = C_ROPE // 2
C_QK = C_NOPE + C_ROPE
ROPE_BASE = 10000.0
D_LEFT_CHUNKS = 8
D_MAX_REL = 128
N_EXPERTS = 16
N_GROUPS = 4
EXPERTS_PER_GROUP = 4
D_EXPERT = 256
LOG2E = math.log2(math.e)

LANES = 128
ATT_BLOCK = 512
ROW_BLOCK = 256
IN_PROJ_ROWS = 512
MERGE_ROWS = 512
MERGE_ROW_BLOCK = 256
ADALN_COLS = 1024
MOE_ROW_BLOCK = 256
MOE_TOKEN_TILE = 256
MOE_ALIGN = 16
MOE_CHUNK = 32
MOE_LOCAL_ROWS = MOE_TOKEN_TILE + N_GROUPS * MOE_CHUNK
MOE_EXPERT_TILE = 512
CW_GROUP_LANE = 3 * EXPERTS_PER_GROUP
BAND_TQ = 256
BAND_LEFT = D_LEFT_CHUNKS * CHUNK
BAND_WIN = BAND_LEFT + BAND_TQ
PAIR_W = 2 * HEAD_DIM
VMEM_LIMIT = 56 * 1024 * 1024


def _cparams(sem):
    return pltpu.CompilerParams(dimension_semantics=sem, vmem_limit_bytes=VMEM_LIMIT)


def _dot(a, b):
    return jnp.dot(a, b, preferred_element_type=F32)


def _dot_nt(a, b):
    return lax.dot_general(a, b, (((1,), (1,)), ((), ())), preferred_element_type=F32)


def _sigmoid(x):
    return 1.0 / (1.0 + jnp.exp(-x))


def _rms(x, g):
    return x * lax.rsqrt(jnp.mean(x * x, axis=-1, keepdims=True) + EPS) * g


def _mod_kernel(c_ref, w_ref, b_ref, o_ref):
    c = c_ref[...]
    cond = c * _sigmoid(c)
    o_ref[0] = jnp.dot(cond, w_ref[0], preferred_element_type=F32,
                       precision=lax.Precision.HIGHEST) + b_ref[0]


def _modulation(c, w_ada, b_ada):
    L, D, D6 = w_ada.shape
    B = c.shape[0]
    tn = ADALN_COLS
    return pl.pallas_call(
        _mod_kernel,
        out_shape=jax.ShapeDtypeStruct((L, B, D6), F32),
        grid=(L, D6 // tn),
        in_specs=[pl.BlockSpec((B, D), lambda l, j: (0, 0)),
                  pl.BlockSpec((1, D, tn), lambda l, j: (l, 0, j)),
                  pl.BlockSpec((1, 1, tn), lambda l, j: (l, 0, j))],
        out_specs=pl.BlockSpec((1, B, tn), lambda l, j: (l, 0, j)),
        name="adaln_mod",
        compiler_params=_cparams(("arbitrary", "arbitrary")),
    )(c, w_ada, b_ada.reshape(L, 1, D6))


def _rope_kernel(ang_ref, cos_ref, sin_ref):
    a = ang_ref[0]
    cos_ref[0] = jnp.cos(a)
    sin_ref[0] = jnp.sin(a)


def _rope_tables(positions):
    B, S = positions.shape
    inv_freq = jnp.power(ROPE_BASE, -jnp.arange(C_HALF, dtype=F32) / C_HALF)
    ang = positions.astype(F32)[:, :, None] * inv_freq[None, None, :]
    rep = LANES // C_HALF
    ang = ang.reshape(B, S // rep, LANES)
    spec = pl.BlockSpec((1, S // rep, LANES), lambda b: (b, 0, 0))
    cos, sin = pl.pallas_call(
        _rope_kernel,
        out_shape=(jax.ShapeDtypeStruct(ang.shape, F32),) * 2,
        grid=(B,),
        in_specs=[spec],
        out_specs=(spec, spec),
        name="rope_tables",
        compiler_params=_cparams(("arbitrary",)),
    )(ang)
    sign = jnp.where(jnp.arange(LANES) < LANES // 2, -1.0, 1.0).astype(F32)
    cosl = jnp.tile(cos.reshape(B, S, C_HALF), (1, 1, rep))
    sinl = jnp.tile(sin.reshape(B, S, C_HALF), (1, 1, rep)) * sign
    return cosl, sinl


def _rope(t, cos, sin):
    return t * cos + pltpu.roll(t, LANES // 2, 1) * sin


def _p1_kernel(x_ref, mod_ref, ng_ref, w1_ref, wfa_ref, fb_ref, gv_ref, wuq_ref, wukv_ref,
               bd_ref, mp_ref, cos_ref, sin_ref,
               qa_ref, ka_ref, va_ref, lf_ref, qb_ref, kb_ref, vb_ref,
               qc_ref, kc_ref, vc_ref, qd_ref, kd_ref, vd_ref):
    x = x_ref[0]
    sh = mod_ref[0, 0:1, :]
    sc = mod_ref[0, 1:2, :]
    hb = (_rms(x, ng_ref[...]) * (1.0 + sc) + sh).astype(BF16)
    W = MIX_W

    def mm(c0, c1):
        return _dot(hb, w1_ref[:, c0:c1])

    def head_norm(t, g):
        ss = _dot((t * t).astype(BF16), bd_ref[...])
        return (t * lax.rsqrt(ss * (1.0 / HEAD_DIM) + EPS) * g).astype(BF16)

    r_a = mm(0, 3 * W)
    fa = _dot_nt(wfa_ref[...], hb) + fb_ref[...]
    r_b = mm(3 * W, 6 * W)

    qa_ref[0] = head_norm(r_a[:, 0:W], gv_ref[0:1, 0:W])
    ka_ref[0] = head_norm(r_a[:, W:2 * W], gv_ref[1:2, 0:W])
    va_ref[0] = r_a[:, 2 * W:3 * W].astype(BF16)
    lf_ref[0] = jnp.minimum(fa, 0.0) - jnp.log(1.0 + jnp.exp(-jnp.abs(fa)))
    r_d = mm(6 * W, 9 * W)

    qb_ref[0] = r_b[:, 0:W].astype(BF16)
    kb_ref[0] = r_b[:, W:2 * W].astype(BF16)
    vb_ref[0] = r_b[:, 2 * W:3 * W].astype(BF16)
    c0 = 9 * W
    r = mm(c0, c0 + C_Q_RANK + C_KV_RANK + LANES)

    qd_ref[0] = head_norm(r_d[:, 0:W], gv_ref[2:3, 0:W])
    kd_ref[0] = head_norm(r_d[:, W:2 * W], gv_ref[3:4, 0:W])
    vd_ref[0] = r_d[:, 2 * W:3 * W].astype(BF16)

    cos = cos_ref[0]
    sin = sin_ref[0]
    cq = _rms(r[:, 0:C_Q_RANK], gv_ref[4:5, 0:C_Q_RANK]).astype(BF16)
    ckv = _rms(r[:, C_Q_RANK:C_Q_RANK + C_KV_RANK], gv_ref[5:6, 0:C_KV_RANK]).astype(BF16)
    kr = _rope(r[:, C_Q_RANK + C_KV_RANK:], cos, sin)
    qu = _dot(cq, wuq_ref[...])
    kvu = _dot(ckv, wukv_ref[...])
    vc_ref[0] = kvu[:, 2 * LANES:].astype(BF16)
    for p in range(2):
        qp = jnp.concatenate([qu[:, p * 2 * LANES:p * 2 * LANES + LANES],
                              _rope(qu[:, p * 2 * LANES + LANES:(p + 1) * 2 * LANES], cos, sin)], axis=1)
        kp = jnp.concatenate([kvu[:, p * LANES:(p + 1) * LANES], kr], axis=1)
        for t, gr, dst in ((qp, 6, qc_ref), (kp, 7, kc_ref)):
            ss = _dot((t * t).astype(BF16), mp_ref[p])
            tn = t * lax.rsqrt(ss * (1.0 / C_QK) + EPS) * gv_ref[gr:gr + 1, p * 2 * LANES:(p + 1) * 2 * LANES]
            dst[0, :, p * 2 * LANES:(p + 1) * 2 * LANES] = tn.astype(BF16)


def _p1(x, mod, ng, lw, cosl, sinl, tm):
    B, S, D = x.shape
    W = MIX_W
    bs = lambda w: pl.BlockSpec((1, tm, w), lambda b, i: (b, i, 0))
    full = lambda a: pl.BlockSpec(a.shape, lambda b, i: (0,) * a.ndim)
    o256 = jax.ShapeDtypeStruct((B, S, W), BF16)
    o512 = jax.ShapeDtypeStruct((B, S, 2 * W), BF16)
    out_shape = (o256, o256, o256, jax.ShapeDtypeStruct((B, 8, S), F32),
                 o256, o256, o256, o512, o512, o256, o256, o256, o256)
    out_specs = (bs(W), bs(W), bs(W), pl.BlockSpec((1, 8, tm), lambda b, i: (b, 0, i)),
                 bs(W), bs(W), bs(W), bs(2 * W), bs(2 * W), bs(W), bs(W), bs(W), bs(W))
    consts = (ng, lw["w1"], lw["wfa"], lw["fb"], lw["gv"], lw["wuq"], lw["wukv"], lw["bd"], lw["mp"])
    return pl.pallas_call(
        _p1_kernel,
        out_shape=out_shape,
        grid=(B, S // tm),
        in_specs=[bs(D), pl.BlockSpec((1, 6, D), lambda b, i: (b, 0, 0))]
                 + [full(a) for a in consts] + [bs(LANES), bs(LANES)],
        out_specs=out_specs,
        name="in_proj",
        compiler_params=_cparams(("arbitrary", "arbitrary")),
    )(x, mod, *consts, cosl, sinl)


def _lane_cumsum(x):
    n = x.shape[1]
    idx = lax.broadcasted_iota(jnp.int32, x.shape, 1)
    k = 1
    while k < n:
        x = x + jnp.where(idx >= k, pltpu.roll(x, k, 1), 0.0)
        k *= 2
    return x


def _ones_outside(own, v):
    return jnp.where(own[:v.shape[0]], v, jnp.ones_like(v))


def _normalize_pair(acc0, acc1, lane):
    half = PAIR_W // 2
    o0 = acc0 / pltpu.roll(acc0, half, 1)
    o1 = acc1 / pltpu.roll(acc1, half, 1)
    return jnp.where(lane < HEAD_DIM, o0, o1)


def _softmax_attn_kernel(mode, S, *refs):
    T = ATT_BLOCK
    if mode == "fox":
        q_ref, k_ref, v_ref, hm_ref, lf_ref, o_ref, cf_ref = refs
        cf = _lane_cumsum(lf_ref[0]) * LOG2E
        for h in range(N_HEADS):
            cf_ref[h] = cf[h:h + 1, :]
    else:
        q_ref, k_ref, v_ref, hm_ref, o_ref = refs
    cw = q_ref.shape[2] // 2
    row = lax.broadcasted_iota(jnp.int32, (T, T), 0)
    col = lax.broadcasted_iota(jnp.int32, (T, T), 1)
    if mode == "fox":
        dmask = col <= row
    else:
        dmask = (col // CHUNK) <= (row // CHUNK)
    lane = lax.broadcasted_iota(jnp.int32, (T, PAIR_W), 1)
    own = [lane // HEAD_DIM == a for a in range(2)]
    heads = [(p, a) for p in range(2) for a in range(2)]

    def q_block(qi, _):
        q0 = pl.multiple_of(qi * T, T)
        qm = [q_ref[0, pl.ds(q0, T), p * cw:(p + 1) * cw] * hm_ref[p, a:a + 1, :] for p, a in heads]

        def step(kj, carry, diag):
            k0 = pl.multiple_of(kj * T, T)
            kt = [k_ref[0, pl.ds(k0, T), p * cw:(p + 1) * cw] for p in range(2)]
            vt = [v_ref[0, pl.ds(k0, T), p * PAIR_W:(p + 1) * PAIR_W] for p in range(2)]
            s = [_dot_nt(qm[h], kt[p]) for h, (p, a) in enumerate(heads)]
            if mode == "fox":
                s = [s[h] - cf_ref[h, :, pl.ds(k0, T)] for h in range(N_HEADS)]
            if diag:
                s = [jnp.where(dmask, sc, NEG_INF) for sc in s]
            m_new = [jnp.maximum(c[0], jnp.max(sc, axis=-1, keepdims=True)) for c, sc in zip(carry, s)]
            pe = [jnp.exp2(sc - mn).astype(BF16) for sc, mn in zip(s, m_new)]
            acc = [jnp.exp2(c[0] - mn) * c[1] + _dot(p_, _ones_outside(own[a], vt[p]))
                   for (p, a), c, mn, p_ in zip(heads, carry, m_new, pe)]
            return tuple(zip(m_new, acc))

        init = tuple((jnp.full((T, 1), NEG_INF, F32), jnp.zeros((T, PAIR_W), F32)) for _ in range(N_HEADS))
        carry = lax.fori_loop(0, qi, lambda kj, c: step(kj, c, False), init)
        carry = step(qi, carry, True)
        for p in range(2):
            o = _normalize_pair(carry[2 * p][1], carry[2 * p + 1][1], lane)
            o_ref[0, pl.ds(q0, T), p * PAIR_W:(p + 1) * PAIR_W] = o.astype(o_ref.dtype)
        return 0

    lax.fori_loop(0, S // T, q_block, 0)


def _softmax_attn(mode, q, k, v, hm, lf=None):
    B, S, _ = v.shape
    spec = lambda w: pl.BlockSpec((1, S, w), lambda b: (b, 0, 0))
    in_specs = [spec(q.shape[2]), spec(q.shape[2]), spec(MIX_W), pl.BlockSpec(hm.shape, lambda b: (0, 0, 0))]
    args = [q, k, v, hm]
    scratch = []
    if mode == "fox":
        in_specs.append(pl.BlockSpec((1, 8, S), lambda b: (b, 0, 0)))
        args.append(lf)
        scratch = [pltpu.VMEM((N_HEADS, 1, S), F32)]
    return pl.pallas_call(
        functools.partial(_softmax_attn_kernel, mode, S),
        out_shape=jax.ShapeDtypeStruct((B, S, MIX_W), BF16),
        grid=(B,),
        in_specs=in_specs,
        out_specs=spec(MIX_W),
        scratch_shapes=scratch,
        name="attn_" + mode,
        compiler_params=_cparams(("arbitrary",)),
    )(*args)


def _band_kernel(S, q_ref, k_ref, v_ref, hm_ref, bias_ref, o_ref):
    lane = lax.broadcasted_iota(jnp.int32, (BAND_TQ, PAIR_W), 1)
    vlane = lax.broadcasted_iota(jnp.int32, (BAND_WIN, PAIR_W), 1)
    own = [vlane // HEAD_DIM == a for a in range(2)]

    def q_block(qi, _):
        q0 = pl.multiple_of(qi * BAND_TQ, BAND_TQ)
        k0 = pl.multiple_of(jnp.maximum(q0 - BAND_LEFT, 0), BAND_TQ)
        case = jnp.minimum(qi, BAND_LEFT // BAND_TQ)
        heads = [(p, a) for p in range(2) for a in range(2)]
        ps = [slice(p * PAIR_W, (p + 1) * PAIR_W) for p in range(2)]
        kt = [k_ref[0, pl.ds(k0, BAND_WIN), ps[p]] for p in range(2)]
        vt = [v_ref[0, pl.ds(k0, BAND_WIN), ps[p]] for p in range(2)]
        s = [_dot_nt(q_ref[0, pl.ds(q0, BAND_TQ), ps[p]] * hm_ref[p, a:a + 1, :], kt[p])
             + bias_ref[2 * p + a, case] for p, a in heads]
        pe = [jnp.exp2(sc - jnp.max(sc, axis=-1, keepdims=True)).astype(BF16) for sc in s]
        accs = [_dot(p_, _ones_outside(own[a], vt[p])) for (p, a), p_ in zip(heads, pe)]
        for p in range(2):
            o = _normalize_pair(accs[2 * p], accs[2 * p + 1], lane)
            o_ref[0, pl.ds(q0, BAND_TQ), ps[p]] = o.astype(o_ref.dtype)
        return 0

    lax.fori_loop(0, S // BAND_TQ, q_block, 0)


def _band_attn(q, k, v, hm, bias):
    B, S, _ = v.shape
    spec = pl.BlockSpec((1, S, MIX_W), lambda b: (b, 0, 0))
    return pl.pallas_call(
        functools.partial(_band_kernel, S),
        out_shape=jax.ShapeDtypeStruct((B, S, MIX_W), BF16),
        grid=(B,),
        in_specs=[spec, spec, spec, pl.BlockSpec(hm.shape, lambda b: (0, 0, 0)),
                  pl.BlockSpec(bias.shape, lambda b: (0, 0, 0, 0))],
        out_specs=spec,
        name="attn_band",
        compiler_params=_cparams(("arbitrary",)),
    )(q, k, v, hm, bias)


def _stick_kernel(S, q_ref, k_ref, v_ref, hm_ref, uo_ref, o_ref):
    T = ATT_BLOCK
    row = lax.broadcasted_iota(jnp.int32, (T, T), 0)
    col = lax.broadcasted_iota(jnp.int32, (T, T), 1)
    strict = col < row
    lane = lax.broadcasted_iota(jnp.int32, (T, PAIR_W), 1)

    nrb = T // ROW_BLOCK
    chains = [(p, a, rb) for p in range(2) for a in range(2) for rb in range(nrb)]
    rs = [slice(rb * ROW_BLOCK, (rb + 1) * ROW_BLOCK) for rb in range(nrb)]

    def q_block(qi, _):
        q0 = pl.multiple_of(qi * T, T)
        qm = [q_ref[0, pl.ds(q0 + rb * ROW_BLOCK, ROW_BLOCK), p * PAIR_W:(p + 1) * PAIR_W] * hm_ref[p, a:a + 1, :]
              for p, a, rb in chains]

        def step(kj, carry, diag):
            k0 = pl.multiple_of(kj * T, T)
            kt = [k_ref[0, pl.ds(k0, T), p * PAIR_W:(p + 1) * PAIR_W] for p in range(2)]
            vt = [v_ref[0, pl.ds(k0, T), p * PAIR_W:(p + 1) * PAIR_W] for p in range(2)]
            kw = [(rb + 1) * ROW_BLOCK if diag else T for rb in range(nrb)]
            z = [_dot_nt(qm[c], kt[p][:kw[rb]]) for c, (p, a, rb) in enumerate(chains)]
            sp = []
            for (p, a, rb), zc in zip(chains, z):
                s_ = jnp.maximum(zc, 0.0) + jnp.log2(1.0 + jnp.exp2(-jnp.abs(zc)))
                if diag:
                    s_ = jnp.where(strict[rs[rb], :kw[rb]], s_, 0.0)
                sp.append(s_)
            cum = []
            for (p, a, rb), s_ in zip(chains, sp):
                sb = s_.astype(BF16)
                cum.append([_dot(sb[:, j * LANES:(j + 1) * LANES], uo_ref[...]) for j in range(kw[rb] // LANES)])
            w, runs = [], []
            for (p, a, rb), c, zc, s_, cm in zip(chains, carry, z, sp, cum):
                right = [None] * len(cm)
                run = c[1]
                for j in reversed(range(len(cm))):
                    right[j] = run
                    run = run + cm[j][:, LANES:]
                w_ = jnp.exp2(zc - s_ - jnp.concatenate([x[:, :LANES] for x in cm], axis=1)
                              - jnp.concatenate(right, axis=1))
                if diag:
                    w_ = jnp.where(strict[rs[rb], :kw[rb]], w_, 0.0)
                w.append(w_.astype(BF16))
                runs.append(run)
            accs = [c[0] + _dot(w_, vt[p][:kw[rb]]) for (p, a, rb), c, w_ in zip(chains, carry, w)]
            return tuple(zip(accs, runs))

        init = tuple((jnp.zeros((ROW_BLOCK, PAIR_W), F32), jnp.zeros((ROW_BLOCK, LANES), F32)) for _ in chains)
        carry = step(qi, init, True)
        carry = lax.fori_loop(0, qi, lambda i, c: step(qi - 1 - i, c, False), carry)
        for p in range(2):
            acc = [jnp.concatenate([carry[(2 * p + a) * nrb + rb][0] for rb in range(nrb)], axis=0) for a in range(2)]
            o = jnp.where(lane < HEAD_DIM, acc[0], acc[1])
            o_ref[0, pl.ds(q0, T), p * PAIR_W:(p + 1) * PAIR_W] = o.astype(o_ref.dtype)
        return 0

    lax.fori_loop(0, S // T, q_block, 0)


def _stick_attn(q, k, v, hm, uo):
    B, S, _ = v.shape
    spec = pl.BlockSpec((1, S, MIX_W), lambda b: (b, 0, 0))
    return pl.pallas_call(
        functools.partial(_stick_kernel, S),
        out_shape=jax.ShapeDtypeStruct((B, S, MIX_W), BF16),
        grid=(B,),
        in_specs=[spec, spec, spec, pl.BlockSpec(hm.shape, lambda b: (0, 0, 0)),
                  pl.BlockSpec(uo.shape, lambda b: (0, 0))],
        out_specs=spec,
        name="attn_stick",
        compiler_params=_cparams(("arbitrary",)),
    )(q, k, v, hm, uo)


def _route(scores_t, sel_t):
    G, E = N_GROUPS, EXPERTS_PER_GROUP
    sel = [sel_t[e:e + 1, :] for e in range(G * E)]
    sco = [scores_t[e:e + 1, :] for e in range(G * E)]
    gs = []
    for g in range(G):
        a, b, c, d = sel[E * g:E * g + E]
        gs.append(jnp.maximum(jnp.maximum(jnp.maximum(a + b, a + c), jnp.maximum(a + d, b + c)),
                              jnp.maximum(b + d, c + d)))
    gmax = jnp.maximum(jnp.maximum(gs[0], gs[1]), jnp.maximum(gs[2], gs[3]))
    grp = jnp.where(gs[0] == gmax, 0, jnp.where(gs[1] == gmax, 1, jnp.where(gs[2] == gmax, 2, 3)))

    def pick(rows, j):
        return jnp.where(grp == 0, rows[j], jnp.where(grp == 1, rows[E + j],
                                                      jnp.where(grp == 2, rows[2 * E + j], rows[3 * E + j])))

    ig = [pick(sel, j) for j in range(E)]
    igs = [pick(sco, j) for j in range(E)]

    def first_argmax(v):
        mx = jnp.maximum(jnp.maximum(v[0], v[1]), jnp.maximum(v[2], v[3]))
        return jnp.where(v[0] == mx, 0, jnp.where(v[1] == mx, 1, jnp.where(v[2] == mx, 2, 3)))

    l1 = first_argmax(ig)
    ig2 = [jnp.where(l1 == j, -jnp.inf, ig[j]) for j in range(E)]
    l2 = first_argmax(ig2)

    def take(v, idx):
        return jnp.where(idx == 0, v[0], jnp.where(idx == 1, v[1], jnp.where(idx == 2, v[2], v[3])))

    w1 = take(igs, l1)
    w2 = take(igs, l2)
    den = w1 + w2
    w1 = w1 / den
    w2 = w2 / den
    return grp, [jnp.where(l1 == j, w1, jnp.where(l2 == j, w2, 0.0)) for j in range(E)]


def _k5_kernel(x_ref, mod_ref, ng1_ref, ng2_ref, oa_ref, ob_ref, oc_ref, od_ref,
               wg_ref, wb_ref, wo_ref, wr_ref, rb_ref, x1_ref, h2_ref, cwp_ref, grp_ref):
    D = x_ref.shape[2]
    sh, sc, g_m = mod_ref[0, 0:1, :], mod_ref[0, 1:2, :], mod_ref[0, 2:3, :]
    sh_f, sc_f = mod_ref[0, 3:4, :], mod_ref[0, 4:5, :]
    tr = MERGE_ROW_BLOCK

    def merge(rs):
        hb = (_rms(x_ref[0, rs, :], ng1_ref[...]) * (1.0 + sc) + sh).astype(BF16)
        merged = None
        for b, o_ref in enumerate((oa_ref, ob_ref, oc_ref, od_ref)):
            gate = _sigmoid(_dot(hb, wg_ref[:, b * D:(b + 1) * D]))
            t = gate * _dot(o_ref[0, rs, :], wb_ref[b])
            merged = t if merged is None else merged + t
        return _dot(merged.astype(BF16), wo_ref[...])

    def ffn_norm(rs, y):
        x1 = x_ref[0, rs, :] + g_m * y
        x1_ref[0, rs, :] = x1
        h2 = (_rms(x1, ng2_ref[...]) * (1.0 + sc_f) + sh_f).astype(BF16)
        h2_ref[0, rs, :] = h2
        return _dot(h2, wr_ref[...])

    def route(rs, logits):
        st = _sigmoid(logits).T[0:N_EXPERTS, :]
        grp, cw = _route(st, st + rb_ref[...])
        grp_ref[0, :, rs] = grp
        nr = 16
        rid = lax.broadcasted_iota(jnp.int32, (nr, tr), 0)
        rec = jnp.where(rid == CW_GROUP_LANE, grp.astype(F32), 0.0)
        for j, c in enumerate(cw):
            hi = c.astype(BF16).astype(F32)
            mid = (c - hi).astype(BF16).astype(F32)
            for part, v in enumerate((hi, mid, c - hi - mid)):
                rec = jnp.where(rid == part * EXPERTS_PER_GROUP + j, v, rec)
        cwp_ref[0, rs, :] = jnp.concatenate([rec, jnp.zeros((LANES - nr, tr), F32)], axis=0).T.astype(BF16)

    blocks = [slice(r, r + tr) for r in range(0, x_ref.shape[1], tr)]
    y = merge(blocks[0])
    for i, rs in enumerate(blocks):
        y_next = merge(blocks[i + 1]) if i + 1 < len(blocks) else None
        route(rs, ffn_norm(rs, y))
        y = y_next


def _k5(x, mod, ng1, ng2, oa, ob, oc, od, lw, wr, rb, tm):
    B, S, D = x.shape
    bs = lambda w: pl.BlockSpec((1, tm, w), lambda b, i: (b, i, 0))
    full = lambda a: pl.BlockSpec(a.shape, lambda b, i: (0,) * a.ndim)
    consts = (lw["wg"], lw["wb"], lw["wo"], wr, rb)
    return pl.pallas_call(
        _k5_kernel,
        out_shape=(jax.ShapeDtypeStruct((B, S, D), F32), jax.ShapeDtypeStruct((B, S, D), BF16),
                   jax.ShapeDtypeStruct((B, S, LANES), BF16), jax.ShapeDtypeStruct((B, 1, S), jnp.int32)),
        grid=(B, S // tm),
        in_specs=[bs(D), pl.BlockSpec((1, 6, D), lambda b, i: (b, 0, 0)), full(ng1), full(ng2),
                  bs(MIX_W), bs(MIX_W), bs(MIX_W), bs(MIX_W)] + [full(a) for a in consts],
        out_specs=(bs(D), bs(D), bs(LANES), pl.BlockSpec((1, 1, tm), lambda b, i: (b, 0, i))),
        name="merge_route",
        compiler_params=_cparams(("arbitrary", "arbitrary")),
    )(x, mod, ng1, ng2, oa, ob, oc, od, *consts)


def _swiglu_rows(load_h, nrows, wgu, wd, weight_fn, store_y):
    chunks = [slice(r, r + MOE_ROW_BLOCK) for r in range(0, nrows, MOE_ROW_BLOCK)]

    def up(rs):
        h = load_h(rs)
        return [_dot(h, w) for w in wgu]

    nxt = up(chunks[0])
    for i, rs in enumerate(chunks):
        gu = nxt
        if i + 1 < len(chunks):
            nxt = up(chunks[i + 1])
        acts = []
        for j, gu_j in enumerate(gu):
            g = gu_j[:, :D_EXPERT]
            a = g * _sigmoid(g) * gu_j[:, D_EXPERT:]
            w = weight_fn(rs, j)
            acts.append((a if w is None else a * w).astype(BF16))
        store_y(rs, _dot(acts[0] if len(acts) == 1 else jnp.concatenate(acts, axis=1), wd))


def _moe_plan(grp, n_tokens):
    ntd = n_tokens // MOE_TOKEN_TILE
    g = grp.reshape(ntd, MOE_TOKEN_TILE)
    cnt = jnp.sum((g[:, :, None] == jnp.arange(N_GROUPS, dtype=jnp.int32)).astype(jnp.int32), axis=1)
    c_al = (cnt + MOE_ALIGN - 1) // MOE_ALIGN * MOE_ALIGN
    c_ch = (cnt + MOE_CHUNK - 1) // MOE_CHUNK * MOE_CHUNK
    lo = jnp.cumsum(c_ch, axis=1) - c_ch
    tot = jnp.sum(c_al, axis=0)
    reg = (tot + MOE_CHUNK + MOE_EXPERT_TILE - 1) // MOE_EXPERT_TILE * MOE_EXPERT_TILE
    end = jnp.cumsum(reg)
    goff = (end - reg)[None, :] + jnp.cumsum(c_al, axis=0) - c_al
    start = jnp.arange(_moe_buffer_rows(n_tokens) // MOE_EXPERT_TILE, dtype=jnp.int32) * MOE_EXPERT_TILE
    tile_group = jnp.minimum(jnp.sum((start[:, None] >= end[None, :]).astype(jnp.int32), axis=1), N_GROUPS - 1)
    tile_valid = (start < end[-1]).astype(jnp.int32)
    flat = lambda a: a.reshape(-1).astype(jnp.int32)
    return flat(goff), flat(lo), flat(c_ch // MOE_CHUNK), tile_group.astype(jnp.int32), tile_valid


def _moe_buffer_rows(n_tokens):
    ntd = n_tokens // MOE_TOKEN_TILE
    worst = n_tokens + (MOE_ALIGN - 1) * N_GROUPS * ntd + N_GROUPS * (MOE_CHUNK + MOE_EXPERT_TILE)
    return (worst + MOE_EXPERT_TILE - 1) // MOE_EXPERT_TILE * MOE_EXPERT_TILE


def _segment_copies(i, goff_ref, lo_ref, nch_ref, make_copy, action):
    for g in range(N_GROUPS):
        lo_g = lo_ref[i * N_GROUPS + g]
        go_g = goff_ref[i * N_GROUPS + g]

        def body(c, _):
            action(make_copy(pl.multiple_of(lo_g + c * MOE_CHUNK, MOE_ALIGN),
                             pl.multiple_of(go_g + c * MOE_CHUNK, MOE_ALIGN)))
            return 0

        lax.fori_loop(0, nch_ref[i * N_GROUPS + g], body, 0)


def _dispatch_kernel(goff_ref, lo_ref, nch_ref, h2_ref, cwp_ref, grp_ref, ust_ref, xs_in_ref, xs_ref, xl_ref, sem):
    del xs_in_ref
    i = pl.program_id(0)
    T, R = MOE_TOKEN_TILE, MOE_LOCAL_ROWS
    gid = lax.broadcasted_iota(jnp.int32, (8, T), 0)
    onehot = jnp.where(grp_ref[0] == gid, 1.0, 0.0)
    rank = _dot(onehot.astype(BF16), ust_ref[...])
    lo_rows = jnp.zeros((8, T), F32)
    for g in range(N_GROUPS):
        lo_rows = jnp.where(gid == g, lo_ref[i * N_GROUPS + g].astype(F32), lo_rows)
    lp = jnp.sum(onehot * (rank + lo_rows), axis=0, keepdims=True).astype(jnp.int32)
    perm = jnp.where(lp == lax.broadcasted_iota(jnp.int32, (R, T), 0), 1.0, 0.0).astype(BF16)
    slot = i % 2
    xl_ref[slot] = _dot(perm, jnp.concatenate([h2_ref[...], cwp_ref[...]], axis=1)).astype(BF16)

    def make_copy(s):
        return lambda lo, go: pltpu.make_async_copy(
            xl_ref.at[s, pl.ds(lo, MOE_CHUNK)], xs_ref.at[pl.ds(go, MOE_CHUNK)], sem.at[s])

    @pl.when(i > 0)
    def _():
        _segment_copies(i - 1, goff_ref, lo_ref, nch_ref, make_copy(1 - slot), lambda cp: cp.wait())

    _segment_copies(i, goff_ref, lo_ref, nch_ref, make_copy(slot), lambda cp: cp.start())

    @pl.when(i == pl.num_programs(0) - 1)
    def _():
        _segment_copies(i, goff_ref, lo_ref, nch_ref, make_copy(slot), lambda cp: cp.wait())


def _dispatch(plan, h2, cwp, grp, ust, xs0):
    N, D = h2.shape
    goff, lo, nch, _, _ = plan
    T = MOE_TOKEN_TILE
    rows = _moe_buffer_rows(N)
    assert xs0.shape == (rows, D + LANES) and xs0.dtype == BF16
    grid_spec = pltpu.PrefetchScalarGridSpec(
        num_scalar_prefetch=3,
        grid=(N // T,),
        in_specs=[pl.BlockSpec((T, D), lambda i, *_: (i, 0)),
                  pl.BlockSpec((T, LANES), lambda i, *_: (i, 0)),
                  pl.BlockSpec((1, 1, T), lambda i, *_: (i, 0, 0)),
                  pl.BlockSpec((T, T), lambda i, *_: (0, 0)),
                  pl.BlockSpec(memory_space=pl.ANY)],
        out_specs=pl.BlockSpec(memory_space=pl.ANY),
        scratch_shapes=[pltpu.VMEM((2, MOE_LOCAL_ROWS, D + LANES), BF16), pltpu.SemaphoreType.DMA((2,))],
    )
    return pl.pallas_call(
        _dispatch_kernel,
        out_shape=jax.ShapeDtypeStruct((rows, D + LANES), BF16),
        grid_spec=grid_spec,
        input_output_aliases={7: 0},
        name="moe_dispatch",
        compiler_params=_cparams(("arbitrary",)),
    )(goff, lo, nch, h2, cwp, grp.reshape(N // T, 1, T), ust, xs0)


def _expert_kernel(tg_ref, tv_ref, xs_ref, wgu_ref, wd_ref, ys_ref):
    del tg_ref
    i = pl.program_id(0)
    D = ys_ref.shape[1]
    E = EXPERTS_PER_GROUP
    lane = lax.broadcasted_iota(jnp.int32, (MOE_ROW_BLOCK, LANES), 1)

    @pl.when(tv_ref[i] != 0)
    def _():
        def weight(rs, j):
            rec = xs_ref[rs, D:].astype(F32)
            return jnp.sum(jnp.where((lane % E == j) & (lane < 3 * E), rec, 0.0), axis=-1, keepdims=True)

        def store(rs, y):
            ys_ref[rs, :] = y.astype(ys_ref.dtype)

        _swiglu_rows(lambda rs: xs_ref[rs, :D], MOE_EXPERT_TILE, [wgu_ref[0, j] for j in range(E)],
                     wd_ref[0], weight, store)

    @pl.when(tv_ref[i] == 0)
    def _():
        ys_ref[...] = jnp.zeros_like(ys_ref)


def _experts(plan, xs, wgu, wd):
    rows, XW = xs.shape
    D = XW - LANES
    _, _, _, tile_group, tile_valid = plan
    E = EXPERTS_PER_GROUP
    TE = MOE_EXPERT_TILE
    grid_spec = pltpu.PrefetchScalarGridSpec(
        num_scalar_prefetch=2,
        grid=(rows // TE,),
        in_specs=[pl.BlockSpec((TE, XW), lambda i, tg, tv: (i, 0)),
                  pl.BlockSpec((1, E, D, 2 * D_EXPERT), lambda i, tg, tv: (tg[i], 0, 0, 0)),
                  pl.BlockSpec((1, E * D_EXPERT, D), lambda i, tg, tv: (tg[i], 0, 0))],
        out_specs=pl.BlockSpec((TE, D), lambda i, tg, tv: (i, 0)),
    )
    return pl.pallas_call(
        _expert_kernel,
        out_shape=jax.ShapeDtypeStruct((rows, D), BF16),
        grid_spec=grid_spec,
        name="moe_experts",
        compiler_params=_cparams(("arbitrary",)),
    )(tile_group, tile_valid, xs, wgu, wd)


def _combine_kernel(goff_ref, lo_ref, nch_ref, ys_ref, x1_ref, h2_ref, cwp_ref, mod_ref, lst_ref,
                    wgu_s_ref, wd_s_ref, out_ref, yl_ref, sem):
    i = pl.program_id(0)
    n = pl.num_programs(0)
    T, R = MOE_TOKEN_TILE, MOE_LOCAL_ROWS

    def make_copy(slot):
        return lambda lo, go: pltpu.make_async_copy(
            ys_ref.at[pl.ds(go, MOE_CHUNK)], yl_ref.at[slot, pl.ds(lo, MOE_CHUNK)], sem.at[slot])

    def fetch(t, slot):
        _segment_copies(t, goff_ref, lo_ref, nch_ref, make_copy(slot), lambda cp: cp.start())

    slot = i % 2

    @pl.when(i == 0)
    def _():
        fetch(0, 0)

    @pl.when(i + 1 < n)
    def _():
        fetch(i + 1, 1 - slot)

    lane = lax.broadcasted_iota(jnp.int32, (T, LANES), 1)
    gcol = cwp_ref[:, CW_GROUP_LANE:CW_GROUP_LANE + 1].astype(F32).astype(jnp.int32)
    onehot = jnp.where(gcol == lane, 1.0, 0.0)
    rank = _dot(lst_ref[...], onehot.astype(BF16))
    lo_lanes = jnp.zeros((T, LANES), F32)
    for g in range(N_GROUPS):
        lo_lanes = jnp.where(lane == g, lo_ref[i * N_GROUPS + g].astype(F32), lo_lanes)
    lp = jnp.sum(onehot * (rank + lo_lanes), axis=-1, keepdims=True).astype(jnp.int32)
    perm_t = jnp.where(lp == lax.broadcasted_iota(jnp.int32, (T, R), 1), 1.0, 0.0).astype(BF16)

    acc = []
    _swiglu_rows(lambda rs: h2_ref[rs, :], T, [wgu_s_ref[...]], wd_s_ref[...], lambda rs, j: None,
                 lambda rs, y: acc.append(y))
    shared = acc[0] if len(acc) == 1 else jnp.concatenate(acc, axis=0)

    _segment_copies(i, goff_ref, lo_ref, nch_ref, make_copy(slot), lambda cp: cp.wait())
    used = lo_ref[i * N_GROUPS + N_GROUPS - 1] + nch_ref[i * N_GROUPS + N_GROUPS - 1] * MOE_CHUNK
    row = lax.broadcasted_iota(jnp.int32, (R, out_ref.shape[1]), 0)
    yl = jnp.where(row < used, yl_ref[slot], jnp.zeros((), yl_ref.dtype))
    out_ref[...] = x1_ref[...] + mod_ref[0, 5:6, :] * (shared + _dot(perm_t, yl))


def _combine(plan, ys, x1, h2, cwp, mod, lst, wgu_s, wd_s, S):
    N, D = h2.shape
    goff, lo, nch, _, _ = plan
    T = MOE_TOKEN_TILE
    per_b = S // T
    row = lambda w: pl.BlockSpec((T, w), lambda i, *_: (i, 0))
    grid_spec = pltpu.PrefetchScalarGridSpec(
        num_scalar_prefetch=3,
        grid=(N // T,),
        in_specs=[pl.BlockSpec(memory_space=pl.ANY), row(D), row(D), row(LANES),
                  pl.BlockSpec((1, 6, D), lambda i, *_: (i // per_b, 0, 0)),
                  pl.BlockSpec((T, T), lambda i, *_: (0, 0)),
                  pl.BlockSpec((D, 2 * D_EXPERT), lambda i, *_: (0, 0)),
                  pl.BlockSpec((D_EXPERT, D), lambda i, *_: (0, 0))],
        out_specs=row(D),
        scratch_shapes=[pltpu.VMEM((2, MOE_LOCAL_ROWS, D), BF16), pltpu.SemaphoreType.DMA((2,))],
    )
    return pl.pallas_call(
        _combine_kernel,
        out_shape=jax.ShapeDtypeStruct((N, D), F32),
        grid_spec=grid_spec,
        name="moe_combine",
        compiler_params=_cparams(("arbitrary",)),
    )(goff, lo, nch, ys, x1, h2, cwp, mod, lst, wgu_s, wd_s)


def _const_tables():
    lane = np.arange(MIX_W)
    bd = (lane[:, None] // HEAD_DIM == lane[None, :] // HEAD_DIM).astype(np.float32)
    mp = np.zeros((2, MIX_W, MIX_W), np.float32)
    hm_c = np.zeros((2, 2, MIX_W), np.float32)
    for p in range(2):
        head = np.where(lane < LANES, 2 * p + lane // HEAD_DIM, ((lane - LANES) % HEAD_DIM) // C_HALF)
        mp[p] = head[:, None] == head[None, :]
        for a in range(2):
            hm_c[p, a] = head == 2 * p + a
    hm = np.zeros((2, 2, PAIR_W), np.float32)
    for a in range(2):
        hm[:, a] = (np.arange(PAIR_W) // HEAD_DIM == a)
    j = np.arange(LANES)
    ustrict = (j[:, None] > j[None, :]).astype(np.float32)
    uo = np.concatenate([ustrict, np.ones((LANES, LANES), np.float32)], axis=1)
    t = np.arange(MOE_TOKEN_TILE)
    earlier = (t[:, None] < t[None, :]).astype(np.float32)
    return (jnp.asarray(bd, BF16), jnp.asarray(mp, BF16), jnp.asarray(hm, BF16),
            jnp.asarray(hm_c, BF16), jnp.asarray(uo, BF16), jnp.asarray(earlier, BF16),
            jnp.asarray(earlier.T, BF16))


def _band_bias(rel_bias):
    L, H, _ = rel_bias.shape
    R, Wn = BAND_TQ, BAND_WIN
    ncase = BAND_LEFT // BAND_TQ + 1
    off = (np.arange(ncase) * BAND_TQ)[:, None]
    n = R + Wn
    k = np.arange(n)[None, :]
    dist = np.where(k <= Wn, off - k, off + n - k)
    idx = np.clip(dist, -D_MAX_REL, D_MAX_REL) + D_MAX_REL
    u = rel_bias[:, :, jnp.asarray(idx)]
    bias = jnp.tile(u, (1, 1, 1, R))[..., :R * (n - 1)].reshape(L, H, ncase, R, n - 1)[..., :Wn]
    r = np.arange(R)[None, :, None]
    c = np.arange(Wn)[None, None, :]
    kc = np.floor_divide(c - off[:, :, None], CHUNK)
    rc = r // CHUNK
    vis = (kc <= rc) & (kc >= rc - D_LEFT_CHUNKS)
    return jnp.where(jnp.asarray(vis)[None, None], bias * LOG2E, NEG_INF).astype(F32)


def _pack_weights(w_in, fox_forget_b, fox_q_g, fox_k_g, mla_cq_g, mla_ckv_g, mla_w_uq, mla_w_ukv,
                  mla_q_g, mla_k_g, chunk_q_g, chunk_k_g, w_branch, w_out):
    L, D, _ = w_in.shape
    W = MIX_W
    a0, b0 = 0, 3 * W + N_HEADS
    c0 = b0 + 3 * W
    d0 = c0 + C_Q_RANK + C_KV_RANK + C_ROPE
    g0 = d0 + 3 * W
    scale = HEAD_DIM ** -0.5
    kr = w_in[:, :, c0 + C_Q_RANK + C_KV_RANK:d0]
    w1 = jnp.concatenate([
        w_in[:, :, a0:a0 + 3 * W],
        w_in[:, :, b0:b0 + W] * (scale * LOG2E), w_in[:, :, b0 + W:b0 + 3 * W],
        w_in[:, :, d0:d0 + 3 * W],
        w_in[:, :, c0:c0 + C_Q_RANK + C_KV_RANK],
        jnp.tile(kr[:, :, :C_HALF], (1, 1, N_HEADS)), jnp.tile(kr[:, :, C_HALF:], (1, 1, N_HEADS)),
    ], axis=2).astype(BF16)
    wfa = jnp.pad(jnp.swapaxes(w_in[:, :, 3 * W:3 * W + N_HEADS], 1, 2), ((0, 0), (0, 8 - N_HEADS), (0, 0))).astype(BF16)
    fb = jnp.pad(fox_forget_b, ((0, 0), (0, 8 - N_HEADS))).reshape(L, 8, 1).astype(F32)
    wg = w_in[:, :, g0:].astype(BF16)

    uq = mla_w_uq.reshape(L, C_Q_RANK, N_HEADS, C_QK)
    uq_rope = jnp.concatenate([uq[..., C_NOPE:C_NOPE + C_HALF].reshape(L, C_Q_RANK, N_HEADS * C_HALF),
                               uq[..., C_NOPE + C_HALF:].reshape(L, C_Q_RANK, N_HEADS * C_HALF)], axis=2)
    wuq = jnp.concatenate([uq[:, :, 0, :C_NOPE], uq[:, :, 1, :C_NOPE], uq_rope,
                           uq[:, :, 2, :C_NOPE], uq[:, :, 3, :C_NOPE], uq_rope], axis=2).astype(BF16)
    ukv = mla_w_ukv.reshape(L, C_KV_RANK, N_HEADS, C_NOPE + HEAD_DIM)
    wukv = jnp.concatenate([ukv[..., :C_NOPE].reshape(L, C_KV_RANK, W),
                            ukv[..., C_NOPE:].reshape(L, C_KV_RANK, W)], axis=2).astype(BF16)

    def c_gain(g, s):
        nope = g[:, :C_NOPE]
        rope = jnp.concatenate([jnp.tile(g[:, C_NOPE:C_NOPE + C_HALF], (1, N_HEADS)),
                                jnp.tile(g[:, C_NOPE + C_HALF:], (1, N_HEADS))], axis=1)
        pair = jnp.concatenate([nope, nope, rope], axis=1)
        return jnp.concatenate([pair, pair], axis=1) * s

    def pad512(v):
        return jnp.pad(v, ((0, 0), (0, 2 * W - v.shape[1])))

    gv = jnp.stack([
        pad512(jnp.tile(fox_q_g, (1, N_HEADS)) * (scale * LOG2E)), pad512(jnp.tile(fox_k_g, (1, N_HEADS))),
        pad512(jnp.tile(chunk_q_g, (1, N_HEADS)) * (scale * LOG2E)), pad512(jnp.tile(chunk_k_g, (1, N_HEADS))),
        pad512(mla_cq_g), pad512(mla_ckv_g), c_gain(mla_q_g, C_QK ** -0.5 * LOG2E), c_gain(mla_k_g, 1.0),
    ], axis=1).astype(F32)
    return dict(w1=w1, wfa=wfa, fb=fb, wg=wg, wuq=wuq, wukv=wukv, gv=gv,
                wb=w_branch.astype(BF16), wo=w_out.astype(BF16))


def kernel(x, c, positions, norm_mix_g, norm_ffn_g, w_ada, b_ada, w_in, fox_forget_b, fox_q_g, fox_k_g, mla_cq_g, mla_ckv_g, mla_w_uq, mla_w_ukv, mla_q_g, mla_k_g, chunk_q_g, chunk_k_g, chunk_rel_bias, w_branch, w_out, router_w, router_b, exp_w_gate, exp_w_up, exp_w_down, sh_w_gate, sh_w_up, sh_w_down):
    B, S, D = x.shape
    L = w_in.shape[0]
    assert S % ATT_BLOCK == 0 and S >= BAND_WIN and D % LANES == 0
    assert S % IN_PROJ_ROWS == 0 and S % MERGE_ROWS == 0 and S % MOE_TOKEN_TILE == 0
    tm_p1 = IN_PROJ_ROWS
    tm_k5 = MERGE_ROWS

    bd, mp, hm, hm_c, uo, ust, lst = _const_tables()
    pw = _pack_weights(w_in, fox_forget_b, fox_q_g, fox_k_g, mla_cq_g, mla_ckv_g, mla_w_uq, mla_w_ukv,
                       mla_q_g, mla_k_g, chunk_q_g, chunk_k_g, w_branch, w_out)
    band_bias = _band_bias(chunk_rel_bias)
    wr = jnp.pad(router_w, ((0, 0), (0, LANES - N_EXPERTS))).astype(BF16)
    rb = router_b.reshape(N_EXPERTS, 1).astype(F32)
    wgu = jnp.concatenate([exp_w_gate, exp_w_up], axis=-1).astype(BF16).reshape(
        L, N_GROUPS, EXPERTS_PER_GROUP, D, 2 * D_EXPERT)
    wd = exp_w_down.astype(BF16).reshape(L, N_GROUPS, EXPERTS_PER_GROUP * D_EXPERT, D)
    wgu_s = jnp.concatenate([sh_w_gate, sh_w_up], axis=-1).astype(BF16)
    wd_s = sh_w_down.astype(BF16)

    mod = _modulation(c, w_ada, b_ada).reshape(L, B, 6, D)
    cosl, sinl = _rope_tables(positions)

    xs = jnp.zeros((_moe_buffer_rows(B * S), D + LANES), BF16)
    for l in range(L):
        lw = {k: v[l] for k, v in pw.items()}
        lw["bd"], lw["mp"] = bd, mp
        ng1 = norm_mix_g[l].reshape(1, D)
        ng2 = norm_ffn_g[l].reshape(1, D)
        (qa, ka, va, lf, qb, kb, vb, qc, kc, vc, qd, kd, vd) = _p1(x, mod[l], ng1, lw, cosl, sinl, tm_p1)
        oa = _softmax_attn("fox", qa, ka, va, hm, lf)
        ob = _stick_attn(qb, kb, vb, hm, uo)
        oc = _softmax_attn("mla", qc, kc, vc, hm_c)
        od = _band_attn(qd, kd, vd, hm, band_bias[l])
        x1, h2, cwp, grp = _k5(x, mod[l], ng1, ng2, oa, ob, oc, od, lw, wr, rb, tm_k5)
        h2, cwp = h2.reshape(B * S, D), cwp.reshape(B * S, LANES)
        plan = _moe_plan(grp, B * S)
        xs = _dispatch(plan, h2, cwp, grp, ust, xs)
        ys = _experts(plan, xs, wgu[l], wd[l])
        x = _combine(plan, ys, x1.reshape(B * S, D), h2, cwp, mod[l], lst, wgu_s[l], wd_s[l], S).reshape(B, S, D)
    return x
```
